```python
import math
import jax
import jax.numpy as jnp
from jax import lax
import numpy as np

D_MODEL = 2048
BATCH = 2
SEQ = 4096
DEPTH = 4
DEC_BATCH = 8
DEC_SEQ = 1
PAST_LEN = 16384
PAGE_SIZE = 128

D_BRANCH = 1024
N_BRANCH = 3
D_CONV = D_BRANCH
CONV_W = 3
DK_DA = 64
DV_DA = 2 * DK_DA
H_DA = D_BRANCH // DV_DA
DQK_M = 128
DV_M = 256
H_M = D_BRANCH // DV_M
N_BUCKETS = 32
MAX_DISTANCE = 128
N_GROUPS = 4
EXP_PER_GROUP = 4
N_EXPERTS = N_GROUPS * EXP_PER_GROUP
TOP_K_IN_GROUP = 2
D_EXPERT = 512
Q_BLOCK = 128
M_CHUNK = 128
LN_EPS = 1e-5
IN_SIZES = (D_CONV, D_CONV, D_CONV, H_DA * 2 * DK_DA, H_DA * 2 * DK_DA, H_DA * DV_DA,
            H_M * DQK_M, H_M * DQK_M, H_M * DV_M, H_M * DV_M, H_M, H_M, N_BRANCH * D_MODEL)
IN_SPLITS = tuple(int(v) for v in np.cumsum(IN_SIZES)[:-1])
N_IN = int(sum(IN_SIZES))

kernel_name = 'hybrid_conv_diffattn_mlstm_hmoe_step'


def layer_norm(x, g, b):
    xf = x.astype(jnp.float32)
    mu = jnp.mean(xf, -1, keepdims=True)
    var = jnp.mean(jnp.square(xf - mu), -1, keepdims=True)
    return ((xf - mu) * lax.rsqrt(var + LN_EPS) * g + b).astype(x.dtype)


def rms_norm(x, w):
    xf = x.astype(jnp.float32)
    return (xf * lax.rsqrt(jnp.mean(xf * xf, -1, keepdims=True) + LN_EPS) * w).astype(x.dtype)


def t5_bucket(dist):
    dist = jnp.maximum(dist, 0)
    max_exact = N_BUCKETS // 2
    large = max_exact + (jnp.log(jnp.maximum(dist, 1).astype(jnp.float32) / max_exact)
                         / math.log(MAX_DISTANCE / max_exact) * (N_BUCKETS - max_exact)).astype(jnp.int32)
    large = jnp.minimum(large, N_BUCKETS - 1)
    return jnp.where(dist < max_exact, dist, large)


def short_conv(u, buf, w):
    s = u.shape[1]
    ext = jnp.concatenate([buf.astype(u.dtype), u], axis=1)
    y = ext[:, 0:s] * w[0]
    for j in range(1, CONV_W):
        y = y + ext[:, j:j + s] * w[j]
    return y, ext[:, s:]


def diff_attention(q, k, v, q_pos, k_pos, lam, rel_bias):
    bsz, sq = q.shape[0], q.shape[1]
    blk = Q_BLOCK if sq % Q_BLOCK == 0 else sq
    nb = sq // blk
    q_blocks = jnp.moveaxis(q.reshape(bsz, nb, blk, H_DA, 2, DK_DA), 1, 0)
    pos_blocks = q_pos.reshape(nb, blk)
    scale = DK_DA ** -0.5

    def one_block(args):
        qb, pb = args
        s = jnp.einsum('bqhcd,bkhcd->bchqk', qb, k).astype(jnp.float32) * scale
        dist = pb[:, None] - k_pos[None, :]
        bias = jnp.transpose(rel_bias[t5_bucket(dist)], (2, 0, 1)).astype(jnp.float32)
        s = jnp.where(dist >= 0, s + bias, -jnp.inf)
        p = jax.nn.softmax(s, axis=-1)
        a = p[:, 0] - lam * p[:, 1]
        return jnp.einsum('bhqk,bkhd->bqhd', a.astype(v.dtype), v)

    o = lax.map(one_block, (q_blocks, pos_blocks))
    return jnp.moveaxis(o, 0, 1).reshape(bsz, sq, H_DA, DV_DA)


def mlstm_chunkwise(q, k, v, i_pre, logf, c0, n0, m0):
    bsz, s = q.shape[0], q.shape[1]
    dtype = q.dtype
    L = M_CHUNK if s % M_CHUNK == 0 else s
    nc = s // L

    def to_chunks(t):
        t = jnp.swapaxes(t, 1, 2)
        t = t.reshape((bsz, t.shape[1], nc, L) + t.shape[3:])
        return jnp.moveaxis(t, 2, 0).astype(jnp.float32)

    qc, kc, vc, ic, fc = to_chunks(q), to_chunks(k), to_chunks(v), to_chunks(i_pre), to_chunks(logf)
    causal = jnp.tril(jnp.ones((L, L), bool))

    def step(carry, xs):
        c, n, m = carry
        qq, kk, vv, ii, ff = xs
        b = jnp.cumsum(ff, axis=-1)
        d = jnp.where(causal, b[..., :, None] - b[..., None, :] + ii[..., None, :], -jnp.inf)
        inter = b + m[..., None]
        m_t = jnp.maximum(inter, jnp.max(d, -1))
        w_inter = jnp.exp(inter - m_t)
        qk = jnp.einsum('bhld,bhjd->bhlj', qq, kk) * jnp.exp(d - m_t[..., None])
        num = jnp.einsum('bhlj,bhje->bhle', qk, vv) + w_inter[..., None] * jnp.einsum('bhed,bhld->bhle', c, qq)
        den = jnp.sum(qk, -1) + w_inter * jnp.einsum('bhd,bhld->bhl', n, qq)
        h = num / jnp.maximum(jnp.abs(den), jnp.exp(-m_t))[..., None]
        m_new = m_t[..., -1]
        w_end = jnp.exp(b[..., -1:] - b + ii - m_new[..., None])
        decay = jnp.exp(b[..., -1] + m - m_new)
        c_new = decay[..., None, None] * c + jnp.einsum('bhl,bhle,bhld->bhed', w_end, vv, kk)
        n_new = decay[..., None] * n + jnp.einsum('bhl,bhld->bhd', w_end, kk)
        return (c_new, n_new, m_new), h

    init = (c0.astype(jnp.float32), n0.astype(jnp.float32), m0.astype(jnp.float32))
    (c1, n1, m1), h = lax.scan(step, init, (qc, kc, vc, ic, fc))
    h = jnp.moveaxis(h, 0, 2).reshape(bsz, H_M, s, DV_M)
    h = jnp.swapaxes(h, 1, 2).astype(dtype)
    return h, (c1.astype(dtype), n1.astype(dtype), m1.astype(dtype))


def mixing_sublayer(x, q_pos, conv_buf, c0, n0, m0, past_k, past_v, past_pos,
                    w_in_l, b_in_l, conv_w_l, lam, lam_init, subln_l, mnorm_l,
                    w_branch_l, w_out_l, rel_bias):
    bsz, s, _ = x.shape
    z = jnp.einsum('bsd,dn->bsn', x, w_in_l) + b_in_l
    (c_b, c_c, c_x, a_q, a_k, a_v, m_q, m_k, m_v, m_o, m_i, m_f, g) = jnp.split(z, IN_SPLITS, axis=-1)
    y_conv, new_buf = short_conv(c_c * c_x, conv_buf, conv_w_l)
    y_conv = c_b * y_conv
    q = a_q.reshape(bsz, s, H_DA, 2, DK_DA)
    k = a_k.reshape(bsz, s, H_DA, 2, DK_DA)
    v = a_v.reshape(bsz, s, H_DA, DV_DA)
    if past_k is None:
        k_all, v_all, k_pos = k, v, q_pos
    else:
        k_all = jnp.concatenate([past_k, k], axis=1)
        v_all = jnp.concatenate([past_v, v], axis=1)
        k_pos = jnp.concatenate([past_pos, q_pos])
    o = diff_attention(q, k_all, v_all, q_pos, k_pos, lam, rel_bias)
    o = (rms_norm(o, subln_l) * (1.0 - lam_init)).reshape(bsz, s, H_DA * DV_DA)
    mq = m_q.reshape(bsz, s, H_M, DQK_M)
    mk = m_k.reshape(bsz, s, H_M, DQK_M) * (DQK_M ** -0.5)
    mv = m_v.reshape(bsz, s, H_M, DV_M)
    h, (c1, n1, m1) = mlstm_chunkwise(mq, mk, mv, m_i.astype(jnp.float32),
                                      jax.nn.log_sigmoid(m_f.astype(jnp.float32)), c0, n0, m0)
    h = rms_norm(h, mnorm_l).reshape(bsz, s, H_M * DV_M) * jax.nn.sigmoid(m_o)
    branches = jnp.stack([y_conv, o, h], axis=2)
    proj = jnp.einsum('bsnc,ncd->bsnd', branches, w_branch_l)
    gates = jax.nn.sigmoid(g.reshape(bsz, s, N_BRANCH, D_MODEL))
    merged = jnp.sum(gates * proj, axis=2)
    out = jnp.einsum('bsd,de->bse', merged, w_out_l)
    return out, (k.reshape(bsz, s, H_DA, 2 * DK_DA), v, new_buf, c1, n1, m1)


def hier_moe(x, w_rg, b_rg, w_re, b_re, w_gate, w_up, w_down):
    bsz, s, d = x.shape
    t = bsz * s
    xf = x.reshape(t, d)
    rows = jnp.arange(t)
    lg = (xf @ w_rg + b_rg).astype(jnp.float32)
    pg = jax.nn.softmax(lg, -1)
    gsel = jnp.argmax(lg, -1)
    pg_sel = pg[rows, gsel]
    le = (xf @ w_re + b_re).astype(jnp.float32).reshape(t, N_GROUPS, EXP_PER_GROUP)
    pe = jax.nn.softmax(le[rows, gsel], -1)
    top_p, top_i = lax.top_k(pe, TOP_K_IN_GROUP)
    weights = pg_sel[:, None] * top_p / jnp.sum(top_p, -1, keepdims=True)
    expert_idx = gsel[:, None] * EXP_PER_GROUP + top_i
    combine = jnp.einsum('tk,tke->te', weights, jax.nn.one_hot(expert_idx, N_EXPERTS, dtype=jnp.float32))
    hid = jax.nn.silu(jnp.einsum('td,edf->tef', xf, w_gate)) * jnp.einsum('td,edf->tef', xf, w_up)
    y = jnp.einsum('tef,efd->td', hid * combine[..., None].astype(hid.dtype), w_down)
    return y.reshape(bsz, s, d)


def _stack(states, i):
    return jnp.stack([st[i] for st in states])


def setup_inputs(seed: int = 0) -> dict:
    key = jax.random.key(seed)
    ks = jax.random.split(key, 32)
    f32 = jnp.float32

    def nrm(k, shape, scale):
        return scale * jax.random.normal(k, shape, f32)

    n_pages = PAST_LEN // PAGE_SIZE
    n_used = DEC_BATCH * n_pages
    n_pool = n_used + max(1, n_used // 4)
    beta = (8.0 * DEPTH) ** -0.25
    page_table = jax.random.permutation(ks[8], n_pool)[:n_used].reshape(DEC_BATCH, n_pages).astype(jnp.int32)
    b_in = nrm(ks[11], (DEPTH, N_IN), 0.02).at[:, IN_SPLITS[10]:IN_SPLITS[11]].add(3.0)
    return dict(
        x_prompt=nrm(ks[0], (BATCH, SEQ, D_MODEL), 1.0),
        x_sample=nrm(ks[1], (DEC_BATCH, DEC_SEQ, D_MODEL), 1.0),
        cache_k=nrm(ks[2], (DEPTH, n_pool, PAGE_SIZE, H_DA, 2 * DK_DA), 1.0),
        cache_v=nrm(ks[3], (DEPTH, n_pool, PAGE_SIZE, H_DA, DV_DA), 1.0),
        state_conv=nrm(ks[4], (DEPTH, DEC_BATCH, CONV_W - 1, D_CONV), 1.0),
        state_mlstm_c=nrm(ks[5], (DEPTH, DEC_BATCH, H_M, DV_M, DQK_M), 0.3),
        state_mlstm_n=jnp.abs(nrm(ks[6], (DEPTH, DEC_BATCH, H_M, DQK_M), 0.3)),
        state_mlstm_m=nrm(ks[7], (DEPTH, DEC_BATCH, H_M), 1.0),
        page_table=page_table,
        rel_bias=nrm(ks[9], (N_BUCKETS, H_DA), 0.5),
        w_in=nrm(ks[10], (DEPTH, D_MODEL, N_IN), D_MODEL ** -0.5),
        b_in=b_in,
        conv_w=nrm(ks[12], (DEPTH, CONV_W, D_CONV), CONV_W ** -0.5),
        attn_lambda=nrm(ks[13], (DEPTH, 4, DK_DA), 0.1),
        attn_subln=1.0 + nrm(ks[14], (DEPTH, DV_DA), 0.02),
        mlstm_norm=1.0 + nrm(ks[15], (DEPTH, DV_M), 0.02),
        w_branch=nrm(ks[16], (DEPTH, N_BRANCH, D_BRANCH, D_MODEL), D_BRANCH ** -0.5),
        w_out=nrm(ks[17], (DEPTH, D_MODEL, D_MODEL), beta * D_MODEL ** -0.5),
        ln_g=1.0 + nrm(ks[18], (DEPTH, 2, D_MODEL), 0.02),
        ln_b=nrm(ks[19], (DEPTH, 2, D_MODEL), 0.02),
        w_router_group=nrm(ks[20], (DEPTH, D_MODEL, N_GROUPS), D_MODEL ** -0.5),
        b_router_group=nrm(ks[21], (DEPTH, N_GROUPS), 0.01),
        w_router_expert=nrm(ks[22], (DEPTH, D_MODEL, N_EXPERTS), D_MODEL ** -0.5),
        b_router_expert=nrm(ks[23], (DEPTH, N_EXPERTS), 0.01),
        w_gate=nrm(ks[24], (DEPTH, N_EXPERTS, D_MODEL, D_EXPERT), D_MODEL ** -0.5),
        w_up=nrm(ks[25], (DEPTH, N_EXPERTS, D_MODEL, D_EXPERT), D_MODEL ** -0.5),
        w_down=nrm(ks[26], (DEPTH, N_EXPERTS, D_EXPERT, D_MODEL), beta * D_EXPERT ** -0.5),
    )


def reference(x_prompt, x_sample, cache_k, cache_v, state_conv, state_mlstm_c, state_mlstm_n, state_mlstm_m,
              page_table, rel_bias, w_in, b_in, conv_w, attn_lambda, attn_subln, mlstm_norm, w_branch, w_out,
              ln_g, ln_b, w_router_group, b_router_group, w_router_expert, b_router_expert, w_gate, w_up, w_down):
    f32 = jnp.float32
    alpha = (2.0 * DEPTH) ** 0.25
    bp, sp, _ = x_prompt.shape
    bs, ss, _ = x_sample.shape
    past_len = page_table.shape[1] * cache_k.shape[2]
    pos_p = jnp.arange(sp, dtype=jnp.int32)
    pos_past = jnp.arange(past_len, dtype=jnp.int32)
    pos_s = past_len + jnp.arange(ss, dtype=jnp.int32)
    zero_buf = jnp.zeros((bp, CONV_W - 1, D_CONV), x_prompt.dtype)
    zero_c = jnp.zeros((bp, H_M, DV_M, DQK_M), f32)
    zero_n = jnp.zeros((bp, H_M, DQK_M), f32)
    zero_m = jnp.zeros((bp, H_M), f32)
    xp, xs = x_prompt, x_sample
    st_p, st_s = [], []
    for l in range(DEPTH):
        lam_init = 0.8 - 0.6 * math.exp(-0.3 * l)
        lv = attn_lambda[l].astype(f32)
        lam = jnp.exp(jnp.sum(lv[0] * lv[1])) - jnp.exp(jnp.sum(lv[2] * lv[3])) + lam_init
        shared = (w_in[l], b_in[l], conv_w[l], lam, lam_init, attn_subln[l], mlstm_norm[l],
                  w_branch[l], w_out[l], rel_bias)
        moe = (w_router_group[l], b_router_group[l], w_router_expert[l], b_router_expert[l],
               w_gate[l], w_up[l], w_down[l])
        mix, st = mixing_sublayer(xp, pos_p, zero_buf, zero_c, zero_n, zero_m, None, None, None, *shared)
        xp = layer_norm(alpha * xp + mix, ln_g[l, 0], ln_b[l, 0])
        xp = layer_norm(alpha * xp + hier_moe(xp, *moe), ln_g[l, 1], ln_b[l, 1])
        st_p.append(st)
        past_k = cache_k[l, page_table].reshape(bs, past_len, H_DA, 2, DK_DA)
        past_v = cache_v[l, page_table].reshape(bs, past_len, H_DA, DV_DA)
        mix, st = mixing_sublayer(xs, pos_s, state_conv[l], state_mlstm_c[l], state_mlstm_n[l], state_mlstm_m[l],
                                  past_k, past_v, pos_past, *shared)
        xs = layer_norm(alpha * xs + mix, ln_g[l, 0], ln_b[l, 0])
        xs = layer_norm(alpha * xs + hier_moe(xs, *moe), ln_g[l, 1], ln_b[l, 1])
        st_s.append(st)
    return (xp, xs,
            _stack(st_p, 0), _stack(st_p, 1), _stack(st_s, 0), _stack(st_s, 1),
            _stack(st_p, 2), _stack(st_s, 2),
            _stack(st_p, 3), _stack(st_s, 3),
            _stack(st_p, 4), _stack(st_s, 4),
            _stack(st_p, 5), _stack(st_s, 5))
```

```python
import functools
import math

import jax
import jax.numpy as jnp
import numpy as np
from jax import lax
from jax.experimental import pallas as pl
from jax.experimental.pallas import tpu as pltpu

D_BRANCH = 1024
N_BRANCH = 3
CONV_W = 3
DK_DA = 64
DV_DA = 2 * DK_DA
H_DA = D_BRANCH // DV_DA
DQK_M = 128
DV_M = 256
H_M = D_BRANCH // DV_M
N_BUCKETS = 32
MAX_DISTANCE = 128
N_GROUPS = 4
EXP_PER_GROUP = 4
N_EXPERTS = N_GROUPS * EXP_PER_GROUP
D_EXPERT = 512
M_CHUNK = 128
LN_EPS = 1e-5

OFF_CONV = 0
OFF_AQ = 3 * D_BRANCH
OFF_AK = OFF_AQ + D_BRANCH
OFF_AV = OFF_AK + D_BRANCH
OFF_MQ = OFF_AV + D_BRANCH
OFF_MK = OFF_MQ + H_M * DQK_M
OFF_MV = OFF_MK + H_M * DQK_M
OFF_MO = OFF_MV + D_BRANCH
OFF_MI = OFF_MO + D_BRANCH
OFF_MF = OFF_MI + H_M
OFF_G = OFF_MF + H_M

F32 = jnp.float32
BF16 = jnp.bfloat16
NEG = -1e30
LANES = 128
V7X_VMEM_BUDGET = 56 * 1024 * 1024
HI = lax.Precision.HIGHEST


def _cparams(sem, vmem_mb):
    return pltpu.CompilerParams(dimension_semantics=sem,
                                vmem_limit_bytes=min(vmem_mb * 1024 * 1024, V7X_VMEM_BUDGET))


def _tile(n, pref):
    t = min(pref, n)
    while n % t:
        t //= 2
    return t


def _rnd(x):
    return x.astype(BF16).astype(F32)


def _rnd_bits(x):
    u = lax.bitcast_convert_type(x, jnp.uint32)
    r = (u + jnp.uint32(0x7FFF) + ((u >> 16) & jnp.uint32(1))) & jnp.uint32(0xFFFF0000)
    return lax.bitcast_convert_type(r, F32)


def _sigmoid(x):
    return 1.0 / (1.0 + jnp.exp(-x))


def _log_sigmoid(x):
    return jnp.minimum(x, 0.0) - jnp.log(1.0 + jnp.exp(-jnp.abs(x)))


def _layer_norm_rows(y, g, b):
    mu = jnp.mean(y, axis=-1, keepdims=True)
    yc = y - mu
    var = jnp.mean(yc * yc, axis=-1, keepdims=True)
    return yc * lax.rsqrt(var + LN_EPS) * g + b


def _proj_kernel(x_ref, w_ref, b_ref, o_ref, wbf_ref):
    @pl.when(pl.program_id(1) == 0)
    def _cast():
        wbf_ref[...] = w_ref[...].astype(BF16)

    acc = jnp.dot(x_ref[...], wbf_ref[...], preferred_element_type=F32)
    o_ref[...] = (acc + b_ref[...]).astype(o_ref.dtype)


def _proj(x_bf, w, lead, col0, ncols, bias, out_dtype, tm=512, tn=1024):
    m, k = x_bf.shape
    tm = _tile(m, tm)
    tn = _tile(math.gcd(ncols, col0) if col0 else ncols, tn)
    nlead = len(lead)
    w_spec = pl.BlockSpec((None,) * nlead + (k, tn), lambda j, i: tuple(lead) + (0, col0 // tn + j))
    return pl.pallas_call(
        _proj_kernel,
        grid=(ncols // tn, m // tm),
        in_specs=[pl.BlockSpec((tm, k), lambda j, i: (i, 0)),
                  w_spec,
                  pl.BlockSpec((1, tn), lambda j, i: (0, j))],
        out_specs=pl.BlockSpec((tm, tn), lambda j, i: (i, j)),
        out_shape=jax.ShapeDtypeStruct((m, ncols), out_dtype),
        scratch_shapes=[pltpu.VMEM((k, tn), BF16)],
        compiler_params=_cparams(("arbitrary", "arbitrary"), 48),
        name="proj",
    )(x_bf, w, bias)


def _conv_kernel(zb_ref, zc_ref, zx_ref, w_ref, buf_ref, y_ref, nb_ref, carry_ref):
    s_idx = pl.program_id(2)

    @pl.when(s_idx == 0)
    def _init():
        carry_ref[...] = buf_ref[...]

    u = zc_ref[...] * zx_ref[...]
    ts = u.shape[0]
    row = lax.broadcasted_iota(jnp.int32, u.shape, 0)
    c0 = carry_ref[0:1, :]
    c1 = carry_ref[1:2, :]
    u1 = jnp.where(row == 0, c1, pltpu.roll(u, 1, 0))
    u2 = jnp.where(row == 0, c0, jnp.where(row == 1, c1, pltpu.roll(u, 2, 0)))
    w = w_ref[...]
    y = u2 * w[0:1, :]
    y = y + u1 * w[1:2, :]
    y = y + u * w[2:3, :]
    y_ref[...] = (zb_ref[...] * y).astype(y_ref.dtype)
    last = zc_ref[ts - 2:ts, :] * zx_ref[ts - 2:ts, :]
    carry_ref[...] = last
    nb_ref[...] = last


def _conv_prompt(z_conv, conv_w, buf, bsz, seq):
    c = D_BRANCH
    tc = _tile(c, 512)
    ts = _tile(seq, 1024)
    ncb = c // tc
    nsb = seq // ts
    z3 = z_conv.reshape(bsz, seq, 3 * c)
    y, nb = pl.pallas_call(
        _conv_kernel,
        grid=(bsz, ncb, nsb),
        in_specs=[pl.BlockSpec((None, ts, tc), lambda b, j, s: (b, s, j)),
                  pl.BlockSpec((None, ts, tc), lambda b, j, s: (b, s, ncb + j)),
                  pl.BlockSpec((None, ts, tc), lambda b, j, s: (b, s, 2 * ncb + j)),
                  pl.BlockSpec((CONV_W, tc), lambda b, j, s: (0, j)),
                  pl.BlockSpec((None, CONV_W - 1, tc), lambda b, j, s: (b, 0, j))],
        out_specs=[pl.BlockSpec((None, ts, tc), lambda b, j, s: (b, s, j)),
                   pl.BlockSpec((None, CONV_W - 1, tc), lambda b, j, s: (b, 0, j))],
        out_shape=[jax.ShapeDtypeStruct((bsz, seq, c), BF16),
                   jax.ShapeDtypeStruct((bsz, CONV_W - 1, c), F32)],
        scratch_shapes=[pltpu.VMEM((CONV_W - 1, tc), F32)],
        compiler_params=_cparams(("arbitrary", "arbitrary", "arbitrary"), 40),
        name="conv",
    )(z3, z3, z3, conv_w, buf)
    return y.reshape(bsz * seq, c), nb


def _t5_bucket(dist):
    dist = jnp.maximum(dist, 0)
    max_exact = N_BUCKETS // 2
    large = max_exact + (jnp.log(jnp.maximum(dist, 1).astype(F32) / max_exact)
                         / math.log(MAX_DISTANCE / max_exact) * (N_BUCKETS - max_exact)).astype(jnp.int32)
    large = jnp.minimum(large, N_BUCKETS - 1)
    return jnp.where(dist < max_exact, dist, large)


def _bias_kernel(rb_ref, bk_ref, o_ref):
    h = pl.program_id(0)
    bk = bk_ref[...]
    acc = jnp.full(bk.shape, NEG, F32)
    for b in range(N_BUCKETS):
        acc = jnp.where(bk == b, rb_ref[b, h], acc)
    o_ref[...] = acc


def _bias_tiles(rel_bias, buckets):
    nk, r, c = buckets.shape
    return pl.pallas_call(
        _bias_kernel,
        grid=(H_DA, nk),
        in_specs=[pl.BlockSpec(memory_space=pltpu.SMEM),
                  pl.BlockSpec((None, r, c), lambda h, k: (k, 0, 0))],
        out_specs=pl.BlockSpec((None, None, r, c), lambda h, k: (h, k, 0, 0)),
        out_shape=jax.ShapeDtypeStruct((H_DA, nk, r, c), F32),
        compiler_params=_cparams(("arbitrary", "arbitrary"), 16),
        name="bias_tiles",
    )(rel_bias, buckets)


def _attn_kernel(it_ref, jt_ref, ph_ref, q_ref, k_ref, vt_ref, bias_ref, lam_ref, sub_ref, o_ref,
                 q1_ref, q2_ref, m1_ref, l1_ref, m2_ref, l2_ref, acc_ref, lam_s, *, lam_init):
    p = pl.program_id(2)
    qi = it_ref[p]
    kj = jt_ref[p]
    phase = ph_ref[p]

    @pl.when((phase == 0) & (kj == 0))
    def _init():
        q = q_ref[...] * (DK_DA ** -0.5)
        lane = lax.broadcasted_iota(jnp.int32, q.shape, 1)
        q1_ref[...] = jnp.where(lane < DK_DA, q, 0.0).astype(BF16)
        q2_ref[...] = jnp.where(lane >= DK_DA, q, 0.0).astype(BF16)
        m1_ref[...] = jnp.full(m1_ref.shape, NEG, F32)
        m2_ref[...] = jnp.full(m2_ref.shape, NEG, F32)
        l1_ref[...] = jnp.zeros(l1_ref.shape, F32)
        l2_ref[...] = jnp.zeros(l2_ref.shape, F32)
        lv = lam_ref[...]
        lam = (jnp.exp(jnp.sum(lv[0:1, :] * lv[1:2, :], axis=1, keepdims=True))
               - jnp.exp(jnp.sum(lv[2:3, :] * lv[3:4, :], axis=1, keepdims=True)) + lam_init)
        lam_s[...] = jnp.broadcast_to(lam, lam_s.shape)

    kb = k_ref[...].astype(BF16)
    bias = bias_ref[...]
    nt = (((1,), (1,)), ((), ()))
    score = lambda qm_ref: lax.dot_general(kb, qm_ref[...], nt, preferred_element_type=F32) + bias

    @pl.when(phase == 0)
    def _stats():
        for qm_ref, m_ref, l_ref in ((q1_ref, m1_ref, l1_ref), (q2_ref, m2_ref, l2_ref)):
            s = score(qm_ref)
            m_old = m_ref[...]
            m_new = jnp.maximum(m_old, jnp.max(s, axis=0, keepdims=True))
            l_ref[...] = jnp.exp(m_old - m_new) * l_ref[...] + jnp.sum(jnp.exp(s - m_new), axis=0, keepdims=True)
            m_ref[...] = m_new

        @pl.when(kj == qi)
        def _close():
            l1_ref[...] = 1.0 / l1_ref[...]
            l2_ref[...] = 1.0 / l2_ref[...]
            acc_ref[...] = jnp.zeros(acc_ref.shape, F32)

    @pl.when(phase == 1)
    def _apply():
        p1 = jnp.exp(score(q1_ref) - m1_ref[...]) * l1_ref[...]
        p2 = jnp.exp(score(q2_ref) - m2_ref[...]) * l2_ref[...]
        a = p1 - lam_s[:, 0:1] * p2
        acc_ref[...] += jnp.dot(vt_ref[...], a.astype(BF16), preferred_element_type=F32)

        @pl.when(kj == qi)
        def _fin():
            ot = acc_ref[...]
            ms = jnp.mean(ot * ot, axis=0, keepdims=True)
            on = ot * lax.rsqrt(ms + LN_EPS) * sub_ref[...] * (1.0 - lam_init)
            o_ref[...] = on.T.astype(o_ref.dtype)


def _attn_prompt(q, k, vt, bias_t, lam_l, subln_col, bsz, seq, lam_init, t):
    nq = seq // t
    steps = [(i, j, ph) for i in range(nq) for ph in range(2) for j in range(i + 1)]
    itab = jnp.asarray([st[0] for st in steps], jnp.int32)
    jtab = jnp.asarray([st[1] for st in steps], jnp.int32)
    ptab = jnp.asarray([st[2] for st in steps], jnp.int32)

    def kind(it, jt, p):
        d = it[p] - jt[p]
        return jnp.where(d == 0, 2, jnp.where(d == 1, 1, 0))

    grid_spec = pltpu.PrefetchScalarGridSpec(
        num_scalar_prefetch=3,
        grid=(bsz, H_DA, len(steps)),
        in_specs=[pl.BlockSpec((None, t, DV_DA), lambda b, h, p, it, jt, ph: (b, it[p], h)),
                  pl.BlockSpec((None, t, DV_DA), lambda b, h, p, it, jt, ph: (b, jt[p], h)),
                  pl.BlockSpec((None, None, DV_DA, t), lambda b, h, p, it, jt, ph: (b, h, 0, jt[p] * ph[p])),
                  pl.BlockSpec((None, None, t, t), lambda b, h, p, it, jt, ph: (h, kind(it, jt, p), 0, 0)),
                  pl.BlockSpec((4, DK_DA), lambda b, h, p, it, jt, ph: (0, 0)),
                  pl.BlockSpec((DV_DA, 1), lambda b, h, p, it, jt, ph: (0, 0))],
        out_specs=pl.BlockSpec((None, t, DV_DA), lambda b, h, p, it, jt, ph: (b, it[p], h)),
        scratch_shapes=[pltpu.VMEM((t, DV_DA), BF16), pltpu.VMEM((t, DV_DA), BF16),
                        pltpu.VMEM((1, t), F32), pltpu.VMEM((1, t), F32),
                        pltpu.VMEM((1, t), F32), pltpu.VMEM((1, t), F32),
                        pltpu.VMEM((DV_DA, t), F32), pltpu.VMEM((1, LANES), F32)])
    return pl.pallas_call(
        functools.partial(_attn_kernel, lam_init=lam_init),
        grid_spec=grid_spec,
        out_shape=jax.ShapeDtypeStruct((bsz, seq, D_BRANCH), BF16),
        compiler_params=_cparams(("arbitrary", "arbitrary", "arbitrary"), 40),
        name="attn",
    )(itab, jtab, ptab, q, k, vt, bias_t, lam_l, subln_col)


def _mlstm_kernel(q_ref, k_ref, v_ref, o_ref, i_ref, f_ref, ct0_ref, n0_ref, m0_ref, nrm_ref,
                  h_ref, ct_ref, n_ref, m_ref, ct_s, n_s, m_s):
    c_idx = pl.program_id(2)

    @pl.when(c_idx == 0)
    def _init():
        ct_s[...] = ct0_ref[...]
        n_s[...] = n0_ref[...]
        m_s[...] = m0_ref[...]

    L = q_ref.shape[0]
    q = q_ref[...]
    k = k_ref[...] * (DQK_M ** -0.5)
    v = v_ref[...]
    i_row = i_ref[...]
    logf = _log_sigmoid(f_ref[...])
    r_io = lax.broadcasted_iota(jnp.int32, (L, L), 0)
    c_io = lax.broadcasted_iota(jnp.int32, (L, L), 1)
    upper = (r_io <= c_io).astype(F32)
    b_row = jnp.dot(jnp.broadcast_to(logf, (8, L)), upper, precision=HI,
                    preferred_element_type=F32)[0:1, :]
    stack = jnp.where(r_io == 0, b_row, jnp.where(r_io == 1, i_row, 0.0))
    stack_t = stack.T
    b_col = stack_t[:, 0:1]
    i_col = stack_t[:, 1:2]
    d = jnp.where(r_io >= c_io, b_col - b_row + i_row, NEG)
    m_prev = m_s[:, 0:1]
    inter = b_col + m_prev
    m_t = jnp.maximum(inter, jnp.max(d, axis=1, keepdims=True))
    w_inter = jnp.exp(inter - m_t)
    qb = q.astype(BF16)
    kb = k.astype(BF16)
    nt = (((1,), (1,)), ((), ()))
    qk = lax.dot_general(qb, kb, nt, preferred_element_type=F32) * jnp.exp(d - m_t)
    ct = ct_s[...]
    num = (jnp.dot(qk.astype(BF16), v.astype(BF16), preferred_element_type=F32)
           + w_inter * jnp.dot(qb, ct.astype(BF16), preferred_element_type=F32))
    n_row = n_s[...]
    den = jnp.sum(qk, axis=1, keepdims=True) + w_inter * jnp.sum(_rnd(q) * _rnd_bits(n_row), axis=1, keepdims=True)
    h = num / jnp.maximum(jnp.abs(den), jnp.exp(-m_t))
    m_new = m_t[L - 1:L, :]
    b_last = b_row[:, L - 1:L]
    w_end = jnp.exp(b_last - b_col + i_col - m_new)
    decay = jnp.exp(b_last + m_prev - m_new)
    ct_new = decay * ct + jnp.dot((w_end * k).T.astype(BF16), v.astype(BF16), preferred_element_type=F32)
    n_new = decay * n_row + jnp.sum(_rnd_bits(w_end) * _rnd(k), axis=0, keepdims=True)
    ct_s[...] = ct_new
    n_s[...] = n_new
    m_s[...] = jnp.broadcast_to(m_new, m_s.shape)
    ct_ref[...] = ct_new
    n_ref[...] = n_new
    m_ref[...] = jnp.broadcast_to(m_new, m_ref.shape)
    hn = h * lax.rsqrt(jnp.mean(h * h, axis=1, keepdims=True) + LN_EPS) * nrm_ref[...]
    h_ref[...] = (hn * _sigmoid(o_ref[...])).astype(h_ref.dtype)


def _mlstm_prompt(mqk, mv, mo, i_rows, f_rows, ct0, n0, m0, mnorm, bsz, seq):
    L = M_CHUNK if seq % M_CHUNK == 0 else seq
    nc = seq // L
    st = lambda b, h, c: (b, h, 0, 0)
    return pl.pallas_call(
        _mlstm_kernel,
        grid=(bsz, H_M, nc),
        in_specs=[pl.BlockSpec((None, L, DQK_M), lambda b, h, c: (b, c, h)),
                  pl.BlockSpec((None, L, DQK_M), lambda b, h, c: (b, c, H_M + h)),
                  pl.BlockSpec((None, L, DV_M), lambda b, h, c: (b, c, h)),
                  pl.BlockSpec((None, L, DV_M), lambda b, h, c: (b, c, h)),
                  pl.BlockSpec((None, None, 1, L), lambda b, h, c: (b, h, 0, c)),
                  pl.BlockSpec((None, None, 1, L), lambda b, h, c: (b, h, 0, c)),
                  pl.BlockSpec((None, None, DQK_M, DV_M), st),
                  pl.BlockSpec((None, None, 1, DQK_M), st),
                  pl.BlockSpec((None, None, 1, LANES), st),
                  pl.BlockSpec((1, DV_M), lambda b, h, c: (0, 0))],
        out_specs=[pl.BlockSpec((None, L, DV_M), lambda b, h, c: (b, c, h)),
                   pl.BlockSpec((None, None, DQK_M, DV_M), st),
                   pl.BlockSpec((None, None, 1, DQK_M), st),
                   pl.BlockSpec((None, None, 1, LANES), st)],
        out_shape=[jax.ShapeDtypeStruct((bsz, seq, D_BRANCH), BF16),
                   jax.ShapeDtypeStruct((bsz, H_M, DQK_M, DV_M), F32),
                   jax.ShapeDtypeStruct((bsz, H_M, 1, DQK_M), F32),
                   jax.ShapeDtypeStruct((bsz, H_M, 1, LANES), F32)],
        scratch_shapes=[pltpu.VMEM((DQK_M, DV_M), F32), pltpu.VMEM((1, DQK_M), F32), pltpu.VMEM((1, LANES), F32)],
        compiler_params=_cparams(("arbitrary", "arbitrary", "arbitrary"), 32),
        name="mlstm",
    )(mqk, mqk, mv, mo, i_rows, f_rows, ct0, n0, m0, mnorm)


def _merge_kernel(b0_ref, b1_ref, b2_ref, g0_ref, g1_ref, g2_ref, w_ref, o_ref, wbf_ref):
    @pl.when(pl.program_id(1) == 0)
    def _cast():
        wbf_ref[...] = w_ref[...].astype(BF16)

    acc = _sigmoid(g0_ref[...]) * jnp.dot(b0_ref[...], wbf_ref[0], preferred_element_type=F32)
    acc = acc + _sigmoid(g1_ref[...]) * jnp.dot(b1_ref[...], wbf_ref[1], preferred_element_type=F32)
    acc = acc + _sigmoid(g2_ref[...]) * jnp.dot(b2_ref[...], wbf_ref[2], preferred_element_type=F32)
    o_ref[...] = acc.astype(o_ref.dtype)


def _merge_prompt(br0, br1, br2, zg, w_branch, layer, tm=512, tn=512):
    m, kb = br0.shape
    d = zg.shape[1] // N_BRANCH
    tm = _tile(m, tm)
    tn = _tile(d, tn)
    nn = d // tn
    bspec = pl.BlockSpec((tm, kb), lambda j, i: (i, 0))
    gspec = lambda n: pl.BlockSpec((tm, tn), lambda j, i: (i, n * nn + j))
    return pl.pallas_call(
        _merge_kernel,
        grid=(nn, m // tm),
        in_specs=[bspec, bspec, bspec, gspec(0), gspec(1), gspec(2),
                  pl.BlockSpec((None, N_BRANCH, kb, tn), lambda j, i: (layer, 0, 0, j))],
        out_specs=pl.BlockSpec((tm, tn), lambda j, i: (i, j)),
        out_shape=jax.ShapeDtypeStruct((m, d), BF16),
        scratch_shapes=[pltpu.VMEM((N_BRANCH, kb, tn), BF16)],
        compiler_params=_cparams(("arbitrary", "arbitrary"), 48),
        name="merge",
    )(br0, br1, br2, zg, zg, zg, w_branch)


def _outproj_ln_kernel(a_ref, w_ref, x_ref, g_ref, b_ref, o_ref, obf_ref, acc_ref, *, alpha, tn):
    j = pl.program_id(1)
    nn = acc_ref.shape[0]
    acc_ref[j] = jnp.dot(a_ref[...], w_ref[...].astype(BF16), preferred_element_type=F32)

    @pl.when(j == nn - 1)
    def _ln():
        d = nn * tn
        ys = [alpha * x_ref[:, c * tn:(c + 1) * tn] + acc_ref[c] for c in range(nn)]
        mu = sum(jnp.sum(y, axis=-1, keepdims=True) for y in ys) / d
        ys = [y - mu for y in ys]
        var = sum(jnp.sum(y * y, axis=-1, keepdims=True) for y in ys) / d
        inv = lax.rsqrt(var + LN_EPS)
        for c in range(nn):
            sl = slice(c * tn, (c + 1) * tn)
            out = ys[c] * inv * g_ref[:, sl] + b_ref[:, sl]
            o_ref[:, sl] = out
            obf_ref[:, sl] = out.astype(BF16)


def _outproj_ln(a_bf, w, layer, x, g, b, alpha, tm=512, tn=512):
    m, k = a_bf.shape
    d = x.shape[1]
    tm = _tile(m, tm)
    tn = _tile(d, tn)
    row = lambda i, j: (i, 0)
    return pl.pallas_call(
        functools.partial(_outproj_ln_kernel, alpha=alpha, tn=tn),
        grid=(m // tm, d // tn),
        in_specs=[pl.BlockSpec((tm, k), row),
                  pl.BlockSpec((None, k, tn), lambda i, j: (layer, 0, j)),
                  pl.BlockSpec((tm, d), row),
                  pl.BlockSpec((1, d), lambda i, j: (0, 0)),
                  pl.BlockSpec((1, d), lambda i, j: (0, 0))],
        out_specs=[pl.BlockSpec((tm, d), row), pl.BlockSpec((tm, d), row)],
        out_shape=[jax.ShapeDtypeStruct((m, d), F32), jax.ShapeDtypeStruct((m, d), BF16)],
        scratch_shapes=[pltpu.VMEM((d // tn, tm, tn), F32)],
        compiler_params=_cparams(("arbitrary", "arbitrary"), 48),
        name="outproj_ln",
    )(a_bf, w, x, g, b)


def _route(z):
    lane = lax.broadcasted_iota(jnp.int32, z.shape, 1)
    lane_f = lane.astype(F32)
    big = float(LANES)
    gmask = lane < N_GROUPS
    lg = jnp.where(gmask, z, NEG)
    gmax = jnp.max(lg, axis=1, keepdims=True)
    gsel = jnp.min(jnp.where(gmask & (lg == gmax), lane_f, big), axis=1, keepdims=True)
    pg_sel = 1.0 / jnp.sum(jnp.where(gmask, jnp.exp(lg - gmax), 0.0), axis=1, keepdims=True)
    lo = N_GROUPS + EXP_PER_GROUP * gsel
    emask = (lane_f >= lo) & (lane_f < lo + EXP_PER_GROUP)
    le = jnp.where(emask, z, NEG)
    emax = jnp.max(le, axis=1, keepdims=True)
    pe = jnp.where(emask, jnp.exp(le - emax), 0.0)
    pe = pe / jnp.sum(pe, axis=1, keepdims=True)
    pm = jnp.where(emask, pe, -1.0)
    p1 = jnp.max(pm, axis=1, keepdims=True)
    i1 = jnp.min(jnp.where(pm == p1, lane_f, big), axis=1, keepdims=True)
    pm2 = jnp.where(lane_f == i1, -1.0, pm)
    p2 = jnp.max(pm2, axis=1, keepdims=True)
    i2 = jnp.min(jnp.where(pm2 == p2, lane_f, big), axis=1, keepdims=True)
    tot = p1 + p2
    w1 = pg_sel * p1 / tot
    w2 = pg_sel * p2 / tot
    wts = jnp.where(lane == 0, w1, jnp.where(lane == 1, w2, 0.0))
    ids = jnp.where(lane == 0, i1 - N_GROUPS, jnp.where(lane == 1, i2 - N_GROUPS, 0.0)).astype(jnp.int32)
    return wts, ids


def _router_kernel(x_ref, w_ref, b_ref, wt_ref, id_ref):
    z = jnp.dot(x_ref[...], w_ref[...].astype(BF16), preferred_element_type=F32) + b_ref[...]
    wts, ids = _route(z)
    wt_ref[...] = wts
    id_ref[...] = ids


def _router(x, w_r, b_r, tm=256):
    m, k = x.shape
    tm = _tile(m, tm)
    return pl.pallas_call(
        _router_kernel,
        grid=(m // tm,),
        in_specs=[pl.BlockSpec((tm, k), lambda i: (i, 0)),
                  pl.BlockSpec((k, LANES), lambda i: (0, 0)),
                  pl.BlockSpec((1, LANES), lambda i: (0, 0))],
        out_specs=[pl.BlockSpec((tm, LANES), lambda i: (i, 0)), pl.BlockSpec((tm, LANES), lambda i: (i, 0))],
        out_shape=[jax.ShapeDtypeStruct((m, LANES), F32), jax.ShapeDtypeStruct((m, LANES), jnp.int32)],
        compiler_params=_cparams(("arbitrary",), 32),
        name="router",
    )(x, w_r, b_r)


def _router_weights(w_rg, b_rg, w_re, b_re):
    k = w_rg.shape[0]
    w = jnp.zeros((k, LANES), F32).at[:, :N_GROUPS].set(w_rg).at[:, N_GROUPS:N_GROUPS + N_EXPERTS].set(w_re)
    b = jnp.zeros((1, LANES), F32).at[0, :N_GROUPS].set(b_rg).at[0, N_GROUPS:N_GROUPS + N_EXPERTS].set(b_re)
    return w, b


def _row_copy(src_hbm, row, dst_ref, r, sem):
    return pltpu.make_async_copy(src_hbm.at[pl.ds(row, 1)], dst_ref.at[pl.ds(r, 1)], sem)


def _gather_kernel(src_ref, x_hbm, o_ref, sem):
    i = pl.program_id(0)
    te = o_ref.shape[0]

    def issue(r, c):
        _row_copy(x_hbm, src_ref[i * te + r], o_ref, r, sem).start()
        return c

    lax.fori_loop(0, te, issue, 0)

    def wait(r, c):
        _row_copy(x_hbm, 0, o_ref, r, sem).wait()
        return c

    lax.fori_loop(0, te, wait, 0)


def _gather_rows(x, row_src, te):
    nr = row_src.shape[0]
    d = x.shape[1]
    grid_spec = pltpu.PrefetchScalarGridSpec(
        num_scalar_prefetch=1,
        grid=(nr // te,),
        in_specs=[pl.BlockSpec(memory_space=pl.ANY)],
        out_specs=pl.BlockSpec((te, d), lambda i, s: (i, 0)),
        scratch_shapes=[pltpu.SemaphoreType.DMA(())])
    return pl.pallas_call(
        _gather_kernel,
        grid_spec=grid_spec,
        out_shape=jax.ShapeDtypeStruct((nr, d), x.dtype),
        compiler_params=_cparams(("arbitrary",), 32),
        name="moe_gather",
    )(row_src, x)


def _expert_kernel(te_ref, first_ref, nused_ref, xs_ref, rw_ref, wg_ref, wu_ref, wd_ref, y_ref,
                   wg_bf, wu_bf, wd_bf):
    i = pl.program_id(0)

    @pl.when(first_ref[i] == 1)
    def _cast():
        wg_bf[...] = wg_ref[...].astype(BF16)
        wu_bf[...] = wu_ref[...].astype(BF16)
        wd_bf[...] = wd_ref[...].astype(BF16)

    @pl.when(i < nused_ref[0])
    def _compute():
        xb = xs_ref[...].astype(BF16)
        g = jnp.dot(xb, wg_bf[...], preferred_element_type=F32)
        u = jnp.dot(xb, wu_bf[...], preferred_element_type=F32)
        hid = g * _sigmoid(g) * u * rw_ref[...]
        y_ref[...] = jnp.dot(hid.astype(BF16), wd_bf[...], preferred_element_type=F32)

    @pl.when(i >= nused_ref[0])
    def _pad():
        y_ref[...] = jnp.zeros(y_ref.shape, y_ref.dtype)


def _expert_mlp(xs, row_w, tile_expert, tile_first, n_used, w_gate, w_up, w_down, layer, te):
    nr, d = xs.shape
    f = w_gate.shape[-1]
    grid_spec = pltpu.PrefetchScalarGridSpec(
        num_scalar_prefetch=3,
        grid=(nr // te,),
        in_specs=[pl.BlockSpec((te, d), lambda i, e, fi, nu: (i, 0)),
                  pl.BlockSpec((te, 1), lambda i, e, fi, nu: (i, 0)),
                  pl.BlockSpec((None, None, d, f), lambda i, e, fi, nu: (layer, e[i], 0, 0)),
                  pl.BlockSpec((None, None, d, f), lambda i, e, fi, nu: (layer, e[i], 0, 0)),
                  pl.BlockSpec((None, None, f, d), lambda i, e, fi, nu: (layer, e[i], 0, 0))],
        out_specs=pl.BlockSpec((te, d), lambda i, e, fi, nu: (i, 0)),
        scratch_shapes=[pltpu.VMEM((d, f), BF16), pltpu.VMEM((d, f), BF16), pltpu.VMEM((f, d), BF16)])
    return pl.pallas_call(
        _expert_kernel,
        grid_spec=grid_spec,
        out_shape=jax.ShapeDtypeStruct((nr, d), F32),
        compiler_params=_cparams(("arbitrary",), 52),
        name="moe_experts",
    )(tile_expert, tile_first, n_used, xs, row_w, w_gate, w_up, w_down)


def _combine_ln_kernel(pos_ref, x_ref, ys_hbm, g_ref, b_ref, o_ref, obf_ref, buf0, buf1, sem, *, alpha):
    i = pl.program_id(0)
    tm = x_ref.shape[0]

    def issue(r, c):
        t = i * tm + r
        _row_copy(ys_hbm, pos_ref[2 * t], buf0, r, sem.at[0]).start()
        _row_copy(ys_hbm, pos_ref[2 * t + 1], buf1, r, sem.at[1]).start()
        return c

    lax.fori_loop(0, tm, issue, 0)

    def wait(r, c):
        _row_copy(ys_hbm, 0, buf0, r, sem.at[0]).wait()
        _row_copy(ys_hbm, 0, buf1, r, sem.at[1]).wait()
        return c

    lax.fori_loop(0, tm, wait, 0)
    y = alpha * x_ref[...] + (buf0[...] + buf1[...])
    out = _layer_norm_rows(y, g_ref[...], b_ref[...])
    o_ref[...] = out
    obf_ref[...] = out.astype(BF16)


def _combine_ln(x, ys, pos, g, b, alpha, tm=256):
    m, d = x.shape
    tm = _tile(m, tm)
    grid_spec = pltpu.PrefetchScalarGridSpec(
        num_scalar_prefetch=1,
        grid=(m // tm,),
        in_specs=[pl.BlockSpec((tm, d), lambda i, p: (i, 0)),
                  pl.BlockSpec(memory_space=pl.ANY),
                  pl.BlockSpec((1, d), lambda i, p: (0, 0)),
                  pl.BlockSpec((1, d), lambda i, p: (0, 0))],
        out_specs=[pl.BlockSpec((tm, d), lambda i, p: (i, 0)), pl.BlockSpec((tm, d), lambda i, p: (i, 0))],
        scratch_shapes=[pltpu.VMEM((tm, d), F32), pltpu.VMEM((tm, d), F32), pltpu.SemaphoreType.DMA((2,))])
    return pl.pallas_call(
        functools.partial(_combine_ln_kernel, alpha=alpha),
        grid_spec=grid_spec,
        out_shape=[jax.ShapeDtypeStruct((m, d), F32), jax.ShapeDtypeStruct((m, d), BF16)],
        compiler_params=_cparams(("arbitrary",), 40),
        name="moe_combine_ln",
    )(pos, x, ys, g, b)


def _dispatch_plan(ids, wts, te):
    t = ids.shape[0]
    a = 2 * t
    nr = a + N_EXPERTS * te
    e_flat = ids.reshape(a)
    w_flat = wts.reshape(a)
    order = jnp.argsort(e_flat, stable=True).astype(jnp.int32)
    e_sorted = e_flat[order]
    counts = jnp.sum(jax.nn.one_hot(e_flat, N_EXPERTS, dtype=jnp.int32), axis=0)
    padded = ((counts + te - 1) // te) * te
    pad_end = jnp.cumsum(padded)
    pad_off = pad_end - padded
    off = jnp.cumsum(counts) - counts
    dest = (pad_off[e_sorted] + jnp.arange(a, dtype=jnp.int32) - off[e_sorted]).astype(jnp.int32)
    row_src = jnp.zeros((nr,), jnp.int32).at[dest].set(order // 2)
    row_w = jnp.zeros((nr,), F32).at[dest].set(w_flat[order]).reshape(nr, 1)
    pos = jnp.zeros((a,), jnp.int32).at[order].set(dest)
    n_used = (pad_end[-1] // te).astype(jnp.int32)
    tile_start = jnp.arange(nr // te, dtype=jnp.int32) * te
    tile_expert = jnp.searchsorted(pad_end, tile_start, side="right").astype(jnp.int32)
    last_e = jnp.max(e_flat).astype(jnp.int32)
    tile_expert = jnp.minimum(tile_expert, last_e)
    prev = jnp.concatenate([jnp.full((1,), -1, jnp.int32), tile_expert[:-1]])
    tile_first = (tile_expert != prev).astype(jnp.int32)
    return row_src, row_w, pos, tile_expert, tile_first, n_used.reshape(1)


def _small_mm_kernel(x_ref, w_ref, b_ref, o_ref):
    o_ref[...] = jnp.dot(x_ref[...].astype(BF16), w_ref[...].astype(BF16), preferred_element_type=F32) + b_ref[...]


def _small_mm(x, w, lead, bias=None, tn=1024):
    r, k = x.shape
    n = w.shape[-1]
    tn = min(tn, n)
    nlead = len(lead)
    if bias is None:
        bias = jnp.zeros((1, n), F32)
    return pl.pallas_call(
        _small_mm_kernel,
        grid=(pl.cdiv(n, tn),),
        in_specs=[pl.BlockSpec((r, k), lambda j: (0, 0)),
                  pl.BlockSpec((None,) * nlead + (k, tn), lambda j: tuple(lead) + (0, j)),
                  pl.BlockSpec((1, tn), lambda j: (0, j))],
        out_specs=pl.BlockSpec((r, tn), lambda j: (0, j)),
        out_shape=jax.ShapeDtypeStruct((r, n), F32),
        compiler_params=_cparams(("arbitrary",), 40),
        name="small_mm",
    )(x, w, bias)


def _small_call(fn, out_shapes, *args, name="small"):
    n_in = len(args)

    def kern(*refs):
        outs = fn(*[r[...] for r in refs[:n_in]])
        for r, o in zip(refs[n_in:], outs):
            r[...] = o

    return pl.pallas_call(kern, out_shape=out_shapes, name=name)(*args)


def _sample_conv_fn(zb, zc, zx, buf0, buf1, w):
    u = zc * zx
    y = buf0 * w[0:1, :]
    y = y + buf1 * w[1:2, :]
    y = y + u * w[2:3, :]
    return zb * y, u


def _sample_mlstm_kernel(q_ref, k_ref, v_ref, vc_ref, o_ref, i_ref, f_ref, c_ref, n_ref, m_ref, nrm_ref,
                         h_ref, c_out, n_out, m_out):
    for hh in range(H_M):
        q = q_ref[hh]
        k = k_ref[hh] * (DQK_M ** -0.5)
        v = v_ref[hh]
        v_col = vc_ref[hh]
        ig = i_ref[hh][:, 0:1]
        logf = _log_sigmoid(f_ref[hh][:, 0:1])
        m_prev = m_ref[hh][:, 0:1]
        c = c_ref[hh]
        n = n_ref[hh]
        inter = logf + m_prev
        m_t = jnp.maximum(inter, ig)
        w_inter = jnp.exp(inter - m_t)
        qk = jnp.sum(q * k, axis=1, keepdims=True) * jnp.exp(ig - m_t)
        cq = lax.dot_general(jnp.broadcast_to(q, (8, DQK_M)).astype(BF16), c.astype(BF16),
                             (((1,), (1,)), ((), ())), preferred_element_type=F32)[0:1, :]
        num = qk * v + w_inter * cq
        den = qk + w_inter * jnp.sum(n * q, axis=1, keepdims=True)
        h = num / jnp.maximum(jnp.abs(den), jnp.exp(-m_t))
        w_end = jnp.exp(ig - m_t)
        decay = jnp.exp(logf + m_prev - m_t)
        c_out[hh] = decay * c + w_end * (v_col * k)
        n_out[hh] = decay * n + w_end * k
        m_out[hh] = jnp.broadcast_to(m_t, (1, LANES))
        hn = h * lax.rsqrt(jnp.mean(h * h, axis=1, keepdims=True) + LN_EPS) * nrm_ref[...]
        h_ref[hh] = hn * _sigmoid(o_ref[hh])


def _sample_mlstm(mq, mk, mv, mo, mi, mf, c0, n0, m0, mnorm):
    r = mq.shape[0]
    mv_col = mv.reshape(r, H_M, DV_M, 1)
    blk = lambda *s: pl.BlockSpec((None,) + s, lambda b: (b,) + (0,) * len(s))
    return pl.pallas_call(
        _sample_mlstm_kernel,
        grid=(r,),
        in_specs=[blk(H_M, 1, DQK_M), blk(H_M, 1, DQK_M), blk(H_M, 1, DV_M), blk(H_M, DV_M, 1), blk(H_M, 1, DV_M),
                  blk(H_M, 1, LANES), blk(H_M, 1, LANES), blk(H_M, DV_M, DQK_M), blk(H_M, 1, DQK_M),
                  blk(H_M, 1, LANES), pl.BlockSpec((1, DV_M), lambda b: (0, 0))],
        out_specs=[blk(H_M, 1, DV_M), blk(H_M, DV_M, DQK_M), blk(H_M, 1, DQK_M), blk(H_M, 1, LANES)],
        out_shape=[jax.ShapeDtypeStruct((r, H_M, 1, DV_M), F32),
                   jax.ShapeDtypeStruct((r, H_M, DV_M, DQK_M), F32),
                   jax.ShapeDtypeStruct((r, H_M, 1, DQK_M), F32),
                   jax.ShapeDtypeStruct((r, H_M, 1, LANES), F32)],
        compiler_params=_cparams(("arbitrary",), 32),
        name="sample_mlstm",
    )(mq, mk, mv, mv_col, mo, mi, mf, c0, n0, m0, mnorm)


def _decode_attn_kernel(pt_ref, q_ref, kn_ref, vn_ref, k_ref, v_ref, bfar_ref, blast_ref, bself_ref,
                        lam_ref, sub_ref, o_ref, m1, l1, m2, l2, acc, *, lam_init):
    sweep = pl.program_id(1)
    p = pl.program_id(2)
    npg = pl.num_programs(2)

    @pl.when((sweep == 0) & (p == 0))
    def _init():
        for m_ref in (m1, m2):
            m_ref[...] = jnp.full(m_ref.shape, NEG, F32)
        for z_ref in (l1, l2):
            z_ref[...] = jnp.zeros(z_ref.shape, F32)

    q = _rnd_bits(q_ref[...] * (DK_DA ** -0.5))
    lane = lax.broadcasted_iota(jnp.int32, q.shape, 1)
    first = lane < DK_DA
    page3 = lambda ref: _rnd(ref[...]).reshape(ref.shape[0] // H_DA, H_DA, DV_DA)

    def update(s, m_ref, l_ref):
        m_old = m_ref[...]
        m_new = jnp.maximum(m_old, jnp.max(s, axis=0))
        l_ref[...] = jnp.exp(m_old - m_new) * l_ref[...] + jnp.sum(jnp.exp(s - m_new[None]), axis=0)
        m_ref[...] = m_new

    def scores(keys, bias):
        prod = keys * q[None]
        s1 = jnp.sum(jnp.where(first[None], prod, 0.0), axis=2, keepdims=True)
        s2 = jnp.sum(jnp.where(first[None], 0.0, prod), axis=2, keepdims=True)
        return s1 + bias, s2 + bias

    bias = jnp.where(p == npg - 1, blast_ref[...], bfar_ref[...])
    s1, s2 = scores(page3(k_ref), bias)
    self_scores = lambda: scores(_rnd_bits(kn_ref[...])[None], bself_ref[...])

    @pl.when(sweep == 0)
    def _stats():
        update(s1, m1, l1)
        update(s2, m2, l2)

        @pl.when(p == npg - 1)
        def _close():
            t1, t2 = self_scores()
            update(t1, m1, l1)
            update(t2, m2, l2)
            l1[...] = 1.0 / l1[...]
            l2[...] = 1.0 / l2[...]
            acc[...] = jnp.zeros(acc.shape, F32)

    @pl.when(sweep == 1)
    def _apply():
        lv = lam_ref[...]
        lam = (jnp.exp(jnp.sum(lv[0:1, :] * lv[1:2, :], axis=1, keepdims=True))
               - jnp.exp(jnp.sum(lv[2:3, :] * lv[3:4, :], axis=1, keepdims=True)) + lam_init)

        def weights(u1, u2):
            w1 = jnp.exp(u1 - m1[...][None]) * l1[...][None]
            w2 = jnp.exp(u2 - m2[...][None]) * l2[...][None]
            return _rnd_bits(w1 - lam[None] * w2)

        acc[...] += jnp.sum(weights(s1, s2) * page3(v_ref), axis=0)

        @pl.when(p == npg - 1)
        def _fin():
            t1, t2 = self_scores()
            o = acc[...] + jnp.sum(weights(t1, t2) * _rnd_bits(vn_ref[...])[None], axis=0)
            on = o * lax.rsqrt(jnp.mean(o * o, axis=1, keepdims=True) + LN_EPS) * sub_ref[...] * (1.0 - lam_init)
            o_ref[...] = on


def _decode_attn(q, k_new, v_new, cache_k, cache_v, page_table, layer, bias_far, bias_last, bias_self,
                 lam_l, subln_row, lam_init):
    r = q.shape[0]
    npg = page_table.shape[1]
    psz = cache_k.shape[2]
    pt = page_table.reshape(-1)
    tok = lambda b, s, p, t: (b, 0, 0)
    cst3 = lambda b, s, p, t: (0, 0, 0)
    kpage = lambda b, s, p, t: (layer, t[b * npg + p], 0, 0)
    vpage = lambda b, s, p, t: (layer, t[b * npg + p * s], 0, 0)
    pool = cache_k.shape[1]
    cache_k = cache_k.reshape(cache_k.shape[0], pool, psz * H_DA, DV_DA)
    cache_v = cache_v.reshape(cache_v.shape[0], pool, psz * H_DA, DV_DA)
    grid_spec = pltpu.PrefetchScalarGridSpec(
        num_scalar_prefetch=1,
        grid=(r, 2, npg),
        in_specs=[pl.BlockSpec((None, H_DA, DV_DA), tok),
                  pl.BlockSpec((None, H_DA, DV_DA), tok),
                  pl.BlockSpec((None, H_DA, DV_DA), tok),
                  pl.BlockSpec((None, None, psz * H_DA, DV_DA), kpage),
                  pl.BlockSpec((None, None, psz * H_DA, DV_DA), vpage),
                  pl.BlockSpec((psz, H_DA, 1), cst3),
                  pl.BlockSpec((psz, H_DA, 1), cst3),
                  pl.BlockSpec((1, H_DA, 1), cst3),
                  pl.BlockSpec((4, DK_DA), lambda b, s, p, t: (0, 0)),
                  pl.BlockSpec((1, DV_DA), lambda b, s, p, t: (0, 0))],
        out_specs=pl.BlockSpec((None, H_DA, DV_DA), tok),
        scratch_shapes=[pltpu.VMEM((H_DA, 1), F32), pltpu.VMEM((H_DA, 1), F32),
                        pltpu.VMEM((H_DA, 1), F32), pltpu.VMEM((H_DA, 1), F32), pltpu.VMEM((H_DA, DV_DA), F32)])
    return pl.pallas_call(
        functools.partial(_decode_attn_kernel, lam_init=lam_init),
        grid_spec=grid_spec,
        out_shape=jax.ShapeDtypeStruct((r, H_DA, DV_DA), F32),
        compiler_params=_cparams(("arbitrary", "arbitrary", "arbitrary"), 32),
        name="decode_attn",
    )(pt, q, k_new, v_new, cache_k, cache_v, bias_far, bias_last, bias_self, lam_l, subln_row)


def _sample_moe_kernel(x_ref, cmb_ref, wg_ref, wu_ref, wd_ref, y_ref):
    e = pl.program_id(0)

    @pl.when(e == 0)
    def _init():
        y_ref[...] = jnp.zeros(y_ref.shape, F32)

    x = x_ref[...].astype(BF16)
    g = jnp.dot(x, wg_ref[...].astype(BF16), preferred_element_type=F32)
    u = jnp.dot(x, wu_ref[...].astype(BF16), preferred_element_type=F32)
    hid = g * _sigmoid(g) * u * cmb_ref[...]
    y_ref[...] += jnp.dot(hid.astype(BF16), wd_ref[...].astype(BF16), preferred_element_type=F32)


def _sample_moe(x, combine_t, w_gate, w_up, w_down, layer):
    r, d = x.shape
    f = w_gate.shape[-1]
    return pl.pallas_call(
        _sample_moe_kernel,
        grid=(N_EXPERTS,),
        in_specs=[pl.BlockSpec((r, d), lambda e: (0, 0)),
                  pl.BlockSpec((None, r, 1), lambda e: (e, 0, 0)),
                  pl.BlockSpec((None, None, d, f), lambda e: (layer, e, 0, 0)),
                  pl.BlockSpec((None, None, d, f), lambda e: (layer, e, 0, 0)),
                  pl.BlockSpec((None, None, f, d), lambda e: (layer, e, 0, 0))],
        out_specs=pl.BlockSpec((r, d), lambda e: (0, 0)),
        out_shape=jax.ShapeDtypeStruct((r, d), F32),
        compiler_params=_cparams(("arbitrary",), 48),
        name="sample_moe",
    )(x, combine_t, w_gate, w_up, w_down)


def _prompt_layer(l, x, x_bf, bsz, seq, bias_t, attn_t, moe_te, p):
    f32 = F32
    alpha = (2.0 * p["depth"]) ** 0.25
    lam_init = 0.8 - 0.6 * math.exp(-0.3 * l)
    t = bsz * seq
    w_in, b_in = p["w_in"], p["b_in"]
    bseg = lambda o, n: b_in[l, o:o + n].reshape(1, n)
    pr = lambda o, n, dt=f32: _proj(x_bf, w_in, (l,), o, n, bseg(o, n), dt)
    z_conv = pr(OFF_CONV, 3 * D_BRANCH)
    a_q = pr(OFF_AQ, D_BRANCH)
    a_k = pr(OFF_AK, D_BRANCH)
    a_v = pr(OFF_AV, D_BRANCH)
    m_qk = pr(OFF_MQ, 2 * H_M * DQK_M)
    m_v = pr(OFF_MV, D_BRANCH)
    m_o = pr(OFF_MO, D_BRANCH)
    z_g = _proj(x_bf, p["w_gates_bf"], (l,), 0, N_BRANCH * x.shape[1], b_in[l, OFF_G:].reshape(1, -1), f32)
    z_if = _proj(x_bf, p["w_if"], (l,), 0, LANES, p["b_if"][l], f32)

    y_conv, conv_buf = _conv_prompt(z_conv, p["conv_w"][l], jnp.zeros((bsz, CONV_W - 1, D_BRANCH), f32), bsz, seq)

    vt = a_v.reshape(bsz, seq, H_DA, DV_DA).transpose(0, 2, 3, 1).astype(BF16)
    o = _attn_prompt(a_q.reshape(bsz, seq, D_BRANCH), a_k.reshape(bsz, seq, D_BRANCH), vt, bias_t,
                     p["attn_lambda"][l], p["attn_subln"][l].reshape(DV_DA, 1), bsz, seq, lam_init, attn_t)
    o = o.reshape(t, D_BRANCH)

    gates = z_if[:, :2 * H_M].reshape(bsz, seq, 2, H_M).transpose(2, 0, 3, 1).reshape(2, bsz, H_M, 1, seq)
    h, ct1, n1, m1 = _mlstm_prompt(
        m_qk.reshape(bsz, seq, D_BRANCH), m_v.reshape(bsz, seq, D_BRANCH), m_o.reshape(bsz, seq, D_BRANCH),
        gates[0], gates[1],
        jnp.zeros((bsz, H_M, DQK_M, DV_M), f32), jnp.zeros((bsz, H_M, 1, DQK_M), f32),
        jnp.zeros((bsz, H_M, 1, LANES), f32), p["mlstm_norm"][l].reshape(1, DV_M), bsz, seq)
    h = h.reshape(t, D_BRANCH)

    merged = _merge_prompt(y_conv, o, h, z_g, p["w_branch"], l)
    x1, x1_bf = _outproj_ln(merged, p["w_out"], l, x, p["ln_g"][l, 0].reshape(1, -1),
                            p["ln_b"][l, 0].reshape(1, -1), alpha)

    w_r, b_r = p["router"][l]
    wts, ids = _router(x1_bf, w_r, b_r)
    row_src, row_w, pos, tile_e, tile_first, n_used = _dispatch_plan(ids[:, :2], wts[:, :2], moe_te)
    xs = _gather_rows(x1, row_src, moe_te)
    ys = _expert_mlp(xs, row_w, tile_e, tile_first, n_used, p["w_gate"], p["w_up"], p["w_down"], l, moe_te)
    x2, x2_bf = _combine_ln(x1, ys, pos, p["ln_g"][l, 1].reshape(1, -1), p["ln_b"][l, 1].reshape(1, -1), alpha)

    state = (a_k.reshape(bsz, seq, H_DA, 2 * DK_DA), a_v.reshape(bsz, seq, H_DA, DV_DA), conv_buf,
             jnp.swapaxes(ct1, -1, -2), n1.reshape(bsz, H_M, DQK_M), m1[:, :, 0, 0])
    return x2, x2_bf, state


def _sample_layer(l, x, conv_state, c0, n0, m0, cache_k, cache_v, page_table, dec_bias, p):
    f32 = F32
    alpha = (2.0 * p["depth"]) ** 0.25
    lam_init = 0.8 - 0.6 * math.exp(-0.3 * l)
    r, d = x.shape
    z = _small_mm(x, p["w_in"], (l,), p["b_in"][l].reshape(1, -1))
    seg = lambda o, n: z[:, o:o + n]
    y_conv, u = _small_call(
        _sample_conv_fn, [jax.ShapeDtypeStruct((r, D_BRANCH), f32)] * 2,
        seg(OFF_CONV, D_BRANCH), seg(OFF_CONV + D_BRANCH, D_BRANCH), seg(OFF_CONV + 2 * D_BRANCH, D_BRANCH),
        conv_state[:, 0], conv_state[:, 1], p["conv_w"][l], name="sample_conv")
    new_buf = jnp.stack([conv_state[:, 1], u], axis=1)

    a_q = seg(OFF_AQ, D_BRANCH).reshape(r, H_DA, DV_DA)
    a_k = seg(OFF_AK, D_BRANCH).reshape(r, H_DA, DV_DA)
    a_v = seg(OFF_AV, D_BRANCH).reshape(r, H_DA, DV_DA)
    bias_far, bias_last, bias_self = dec_bias
    o = _decode_attn(a_q, a_k, a_v, cache_k, cache_v, page_table, l, bias_far, bias_last, bias_self,
                     p["attn_lambda"][l], p["attn_subln"][l].reshape(1, DV_DA), lam_init).reshape(r, D_BRANCH)

    lanes = lambda a: jnp.broadcast_to(a.reshape(r, H_M, 1, 1), (r, H_M, 1, LANES))
    h, c1, n1, m1 = _sample_mlstm(
        seg(OFF_MQ, H_M * DQK_M).reshape(r, H_M, 1, DQK_M), seg(OFF_MK, H_M * DQK_M).reshape(r, H_M, 1, DQK_M),
        seg(OFF_MV, D_BRANCH).reshape(r, H_M, 1, DV_M), seg(OFF_MO, D_BRANCH).reshape(r, H_M, 1, DV_M),
        lanes(seg(OFF_MI, H_M)), lanes(seg(OFF_MF, H_M)),
        c0, n0.reshape(r, H_M, 1, DQK_M), lanes(m0), p["mlstm_norm"][l].reshape(1, DV_M))
    h = h.reshape(r, D_BRANCH)

    projs = [_small_mm(br, p["w_branch"], (l, n)) for n, br in enumerate((y_conv, o, h))]
    g = seg(OFF_G, N_BRANCH * d)

    def merge_fn(g0, g1, g2, p0, p1, p2):
        return (_sigmoid(g0) * p0 + _sigmoid(g1) * p1 + _sigmoid(g2) * p2,)

    merged, = _small_call(merge_fn, [jax.ShapeDtypeStruct((r, d), f32)],
                          g[:, :d], g[:, d:2 * d], g[:, 2 * d:], *projs, name="sample_merge")
    mix = _small_mm(merged, p["w_out"], (l,))

    def ln_fn(xr, yr, gg, bb):
        return (_layer_norm_rows(alpha * xr + yr, gg, bb),)

    x1, = _small_call(ln_fn, [jax.ShapeDtypeStruct((r, d), f32)], x, mix,
                      p["ln_g"][l, 0].reshape(1, -1), p["ln_b"][l, 0].reshape(1, -1), name="sample_ln")
    w_r, b_r = p["router"][l]
    zr = _small_mm(x1, w_r, (), b_r, tn=LANES)
    wts, ids = _small_call(_route, [jax.ShapeDtypeStruct((r, LANES), f32), jax.ShapeDtypeStruct((r, LANES), jnp.int32)],
                           zr, name="sample_route")
    combine = jnp.einsum("tk,tke->te", wts[:, :2], jax.nn.one_hot(ids[:, :2], N_EXPERTS, dtype=f32))
    y = _sample_moe(x1, combine.T.reshape(N_EXPERTS, r, 1), p["w_gate"], p["w_up"], p["w_down"], l)
    x2, = _small_call(ln_fn, [jax.ShapeDtypeStruct((r, d), f32)], x1, y,
                      p["ln_g"][l, 1].reshape(1, -1), p["ln_b"][l, 1].reshape(1, -1), name="sample_ln")
    state = (a_k.reshape(r, 1, H_DA, 2 * DK_DA), a_v.reshape(r, 1, H_DA, DV_DA), new_buf,
             c1, n1.reshape(r, H_M, DQK_M), m1[:, :, 0, 0])
    return x2, state


def kernel(x_prompt, x_sample, cache_k, cache_v, state_conv, state_mlstm_c, state_mlstm_n, state_mlstm_m,
           page_table, rel_bias, w_in, b_in, conv_w, attn_lambda, attn_subln, mlstm_norm, w_branch, w_out,
           ln_g, ln_b, w_router_group, b_router_group, w_router_expert, b_router_expert, w_gate, w_up, w_down):
    depth = w_in.shape[0]
    bp, sp, d = x_prompt.shape
    bs, ss, _ = x_sample.shape
    assert ss == 1, "the sample group decodes one token per sequence"
    psz = cache_k.shape[2]
    past_len = page_table.shape[1] * psz
    assert psz >= MAX_DISTANCE, "only the last page may hold keys closer than MAX_DISTANCE"

    params = dict(
        depth=depth, w_in=w_in, b_in=b_in, conv_w=conv_w, attn_lambda=attn_lambda, attn_subln=attn_subln,
        mlstm_norm=mlstm_norm, w_branch=w_branch, w_out=w_out, ln_g=ln_g, ln_b=ln_b,
        w_gate=w_gate, w_up=w_up, w_down=w_down,
        w_gates_bf=w_in[:, :, OFF_G:].astype(BF16),
        w_if=jnp.pad(w_in[:, :, OFF_MI:OFF_G], ((0, 0), (0, 0), (0, LANES - 2 * H_M))),
        b_if=jnp.pad(b_in[:, OFF_MI:OFF_G], ((0, 0), (0, LANES - 2 * H_M))).reshape(depth, 1, LANES),
        router=[_router_weights(w_router_group[l], b_router_group[l], w_router_expert[l], b_router_expert[l])
                for l in range(depth)],
    )

    attn_t = _tile(sp, 512)
    kk = jnp.arange(attn_t, dtype=jnp.int32)[:, None]
    qq = jnp.arange(attn_t, dtype=jnp.int32)[None, :]
    far = jnp.full((attn_t, attn_t), N_BUCKETS - 1, jnp.int32)
    assert attn_t >= MAX_DISTANCE
    sub = _t5_bucket(qq + attn_t - kk)
    diag = jnp.where(qq >= kk, _t5_bucket(qq - kk), -1)
    bias_t = _bias_tiles(rel_bias, jnp.stack([far, sub, diag]))

    tok = jnp.arange(psz, dtype=jnp.int32)
    bk_last = _t5_bucket(past_len - (past_len - psz + tok))
    bk = jnp.stack([jnp.full((psz,), N_BUCKETS - 1, jnp.int32), bk_last, jnp.zeros((psz,), jnp.int32)])
    dec = _bias_tiles(rel_bias, bk.reshape(3, psz, 1))
    dec = jnp.transpose(dec, (1, 2, 0, 3))
    dec_bias = (dec[0], dec[1], dec[2, 0:1])

    xp = x_prompt.reshape(bp * sp, d)
    xp_bf = xp.astype(BF16)
    xs = x_sample.reshape(bs, d)
    st_p, st_s = [], []
    for l in range(depth):
        xp, xp_bf, st = _prompt_layer(l, xp, xp_bf, bp, sp, bias_t, attn_t, 256, params)
        st_p.append(st)
        xs, st = _sample_layer(l, xs, state_conv[l], state_mlstm_c[l], state_mlstm_n[l], state_mlstm_m[l],
                               cache_k, cache_v, page_table, dec_bias, params)
        st_s.append(st)

    stack = lambda sts, i: jnp.stack([s[i] for s in sts])
    return (xp.reshape(bp, sp, d), xs.reshape(bs, ss, d),
            stack(st_p, 0), stack(st_p, 1), stack(st_s, 0), stack(st_s, 1),
            stack(st_p, 2), stack(st_s, 2),
            stack(st_p, 3), stack(st_s, 3),
            stack(st_p, 4), stack(st_s, 4),
            stack(st_p, 5), stack(st_s, 5))
```

```python
import functools
import math

import jax
import jax.numpy as jnp
import numpy as np
from jax import lax
from jax.experimental import pallas as pl
from jax.experimental.pallas import tpu as pltpu

D_BRANCH = 1024
N_BRANCH = 3
CONV_W = 3
DK_DA = 64
DV_DA = 2 * DK_DA
H_DA = D_BRANCH // DV_DA
DQK_M = 128
DV_M = 256
H_M = D_BRANCH // DV_M
N_BUCKETS = 32
MAX_DISTANCE = 128
N_GROUPS = 4
EXP_PER_GROUP = 4
N_EXPERTS = N_GROUPS * EXP_PER_GROUP
D_EXPERT = 512
M_CHUNK = 128
LN_EPS = 1e-5

OFF_CONV = 0
OFF_AQ = 3 * D_BRANCH
OFF_AK = OFF_AQ + D_BRANCH
OFF_AV = OFF_AK + D_BRANCH
OFF_MQ = OFF_AV + D_BRANCH
OFF_MK = OFF_MQ + H_M * DQK_M
OFF_MV = OFF_MK + H_M * DQK_M
OFF_MO = OFF_MV + D_BRANCH
OFF_MI = OFF_MO + D_BRANCH
OFF_MF = OFF_MI + H_M
OFF_G = OFF_MF + H_M

F32 = jnp.float32
BF16 = jnp.bfloat16
NEG = -1e30
LANES = 128
V7X_VMEM_BUDGET = 56 * 1024 * 1024
HI = lax.Precision.HIGHEST


def _cparams(sem, vmem_mb):
    return pltpu.CompilerParams(dimension_semantics=sem,
                                vmem_limit_bytes=min(vmem_mb * 1024 * 1024, V7X_VMEM_BUDGET))


def _tile(n, pref):
    t = min(pref, n)
    while n % t:
        t //= 2
    return t


def _rnd(x):
    return x.astype(BF16).astype(F32)


def _rnd_bits(x):
    u = lax.bitcast_convert_type(x, jnp.uint32)
    r = (u + jnp.uint32(0x7FFF) + ((u >> 16) & jnp.uint32(1))) & jnp.uint32(0xFFFF0000)
    return lax.bitcast_convert_type(r, F32)


def _sigmoid(x):
    return 1.0 / (1.0 + jnp.exp(-x))


def _log_sigmoid(x):
    return jnp.minimum(x, 0.0) - jnp.log(1.0 + jnp.exp(-jnp.abs(x)))


def _layer_norm_rows(y, g, b):
    mu = jnp.mean(y, axis=-1, keepdims=True)
    yc = y - mu
    var = jnp.mean(yc * yc, axis=-1, keepdims=True)
    return yc * lax.rsqrt(var + LN_EPS) * g + b


def _proj_kernel(x_ref, w_ref, b_ref, o_ref, wbf_ref):
    @pl.when(pl.program_id(1) == 0)
    def _cast():
        wbf_ref[...] = w_ref[0].astype(BF16)

    acc = lax.dot_general(x_ref[...], wbf_ref[...], (((1,), (1,)), ((), ())), preferred_element_type=F32)
    o_ref[...] = (acc + b_ref[...]).astype(o_ref.dtype)


def _proj(x_bf, w_t, layer, row0, ncols, bias, out_dtype, tm=512, tn=1024):
    m, k = x_bf.shape
    tm = _tile(m, tm)
    tn = _tile(ncols, tn)
    return pl.pallas_call(
        _proj_kernel,
        grid=(ncols // tn, m // tm),
        in_specs=[pl.BlockSpec((tm, k), lambda j, i: (i, 0)),
                  pl.BlockSpec((pl.Element(1), pl.Element(tn), pl.Element(k)),
                               lambda j, i: (layer, pl.multiple_of(row0 + j * tn, 8), 0)),
                  pl.BlockSpec((1, tn), lambda j, i: (0, j))],
        out_specs=pl.BlockSpec((tm, tn), lambda j, i: (i, j)),
        out_shape=jax.ShapeDtypeStruct((m, ncols), out_dtype),
        scratch_shapes=[pltpu.VMEM((tn, k), BF16)],
        compiler_params=_cparams(("arbitrary", "arbitrary"), 48),
        name="proj",
    )(x_bf, w_t, bias)


def _conv_kernel(zb_ref, zc_ref, zx_ref, w_ref, buf_ref, y_ref, nb_ref, carry_ref):
    s_idx = pl.program_id(2)

    @pl.when(s_idx == 0)
    def _init():
        carry_ref[...] = buf_ref[...]

    u = zc_ref[...] * zx_ref[...]
    ts = u.shape[0]
    row = lax.broadcasted_iota(jnp.int32, u.shape, 0)
    c0 = carry_ref[0:1, :]
    c1 = carry_ref[1:2, :]
    u1 = jnp.where(row == 0, c1, pltpu.roll(u, 1, 0))
    u2 = jnp.where(row == 0, c0, jnp.where(row == 1, c1, pltpu.roll(u, 2, 0)))
    w = w_ref[...]
    y = u2 * w[0:1, :]
    y = y + u1 * w[1:2, :]
    y = y + u * w[2:3, :]
    y_ref[...] = (zb_ref[...] * y).astype(y_ref.dtype)
    last = zc_ref[ts - 2:ts, :] * zx_ref[ts - 2:ts, :]
    carry_ref[...] = last
    nb_ref[...] = last


def _conv_prompt(z_conv, conv_w, buf, bsz, seq):
    c = D_BRANCH
    tc = _tile(c, 512)
    ts = _tile(seq, 1024)
    ncb = c // tc
    nsb = seq // ts
    z3 = z_conv.reshape(bsz, seq, 3 * c)
    y, nb = pl.pallas_call(
        _conv_kernel,
        grid=(bsz, ncb, nsb),
        in_specs=[pl.BlockSpec((None, ts, tc), lambda b, j, s: (b, s, j)),
                  pl.BlockSpec((None, ts, tc), lambda b, j, s: (b, s, ncb + j)),
                  pl.BlockSpec((None, ts, tc), lambda b, j, s: (b, s, 2 * ncb + j)),
                  pl.BlockSpec((CONV_W, tc), lambda b, j, s: (0, j)),
                  pl.BlockSpec((None, CONV_W - 1, tc), lambda b, j, s: (b, 0, j))],
        out_specs=[pl.BlockSpec((None, ts, tc), lambda b, j, s: (b, s, j)),
                   pl.BlockSpec((None, CONV_W - 1, tc), lambda b, j, s: (b, 0, j))],
        out_shape=[jax.ShapeDtypeStruct((bsz, seq, c), BF16),
                   jax.ShapeDtypeStruct((bsz, CONV_W - 1, c), F32)],
        scratch_shapes=[pltpu.VMEM((CONV_W - 1, tc), F32)],
        compiler_params=_cparams(("arbitrary", "arbitrary", "arbitrary"), 40),
        name="conv",
    )(z3, z3, z3, conv_w, buf)
    return y.reshape(bsz * seq, c), nb


def _t5_bucket(dist):
    dist = jnp.maximum(dist, 0)
    max_exact = N_BUCKETS // 2
    large = max_exact + (jnp.log(jnp.maximum(dist, 1).astype(F32) / max_exact)
                         / math.log(MAX_DISTANCE / max_exact) * (N_BUCKETS - max_exact)).astype(jnp.int32)
    large = jnp.minimum(large, N_BUCKETS - 1)
    return jnp.where(dist < max_exact, dist, large)


def _bias_kernel(rb_ref, bk_ref, o_ref):
    h = pl.program_id(0)
    bk = bk_ref[...]
    acc = jnp.full(bk.shape, NEG, F32)
    for b in range(N_BUCKETS):
        acc = jnp.where(bk == b, rb_ref[b, h], acc)
    o_ref[...] = acc


def _bias_tiles(rel_bias, buckets):
    nk, r, c = buckets.shape
    return pl.pallas_call(
        _bias_kernel,
        grid=(H_DA, nk),
        in_specs=[pl.BlockSpec(memory_space=pltpu.SMEM),
                  pl.BlockSpec((None, r, c), lambda h, k: (k, 0, 0))],
        out_specs=pl.BlockSpec((None, None, r, c), lambda h, k: (h, k, 0, 0)),
        out_shape=jax.ShapeDtypeStruct((H_DA, nk, r, c), F32),
        compiler_params=_cparams(("arbitrary", "arbitrary"), 16),
        name="bias_tiles",
    )(rel_bias, buckets)


def _attn_kernel(it_ref, jt_ref, ph_ref, q_ref, k_ref, vt_ref, bias_ref, lam_ref, sub_ref, o_ref,
                 q1_ref, q2_ref, m1_ref, l1_ref, m2_ref, l2_ref, acc_ref, lam_s, *, lam_init):
    p = pl.program_id(2)
    qi = it_ref[p]
    kj = jt_ref[p]
    phase = ph_ref[p]

    @pl.when((phase == 0) & (kj == 0))
    def _init():
        q = q_ref[...] * (DK_DA ** -0.5)
        lane = lax.broadcasted_iota(jnp.int32, q.shape, 1)
        q1_ref[...] = jnp.where(lane < DK_DA, q, 0.0).astype(BF16)
        q2_ref[...] = jnp.where(lane >= DK_DA, q, 0.0).astype(BF16)
        m1_ref[...] = jnp.full(m1_ref.shape, NEG, F32)
        m2_ref[...] = jnp.full(m2_ref.shape, NEG, F32)
        l1_ref[...] = jnp.zeros(l1_ref.shape, F32)
        l2_ref[...] = jnp.zeros(l2_ref.shape, F32)
        lv = lam_ref[...]
        lam = (jnp.exp(jnp.sum(lv[0:1, :] * lv[1:2, :], axis=1, keepdims=True))
               - jnp.exp(jnp.sum(lv[2:3, :] * lv[3:4, :], axis=1, keepdims=True)) + lam_init)
        lam_s[...] = jnp.broadcast_to(lam, lam_s.shape)

    kb = k_ref[...].astype(BF16)
    bias = bias_ref[...]
    nt = (((1,), (1,)), ((), ()))
    score = lambda qm_ref: lax.dot_general(kb, qm_ref[...], nt, preferred_element_type=F32) + bias

    @pl.when(phase == 0)
    def _stats():
        for qm_ref, m_ref, l_ref in ((q1_ref, m1_ref, l1_ref), (q2_ref, m2_ref, l2_ref)):
            s = score(qm_ref)
            m_old = m_ref[...]
            m_new = jnp.maximum(m_old, jnp.max(s, axis=0, keepdims=True))
            l_ref[...] = jnp.exp(m_old - m_new) * l_ref[...] + jnp.sum(jnp.exp(s - m_new), axis=0, keepdims=True)
            m_ref[...] = m_new

        @pl.when(kj == qi)
        def _close():
            l1_ref[...] = 1.0 / l1_ref[...]
            l2_ref[...] = 1.0 / l2_ref[...]
            acc_ref[...] = jnp.zeros(acc_ref.shape, F32)

    @pl.when(phase == 1)
    def _apply():
        p1 = jnp.exp(score(q1_ref) - m1_ref[...]) * l1_ref[...]
        p2 = jnp.exp(score(q2_ref) - m2_ref[...]) * l2_ref[...]
        a = p1 - lam_s[:, 0:1] * p2
        acc_ref[...] += jnp.dot(vt_ref[...], a.astype(BF16), preferred_element_type=F32)

        @pl.when(kj == qi)
        def _fin():
            ot = acc_ref[...]
            ms = jnp.mean(ot * ot, axis=0, keepdims=True)
            on = ot * lax.rsqrt(ms + LN_EPS) * sub_ref[...] * (1.0 - lam_init)
            o_ref[...] = on.T.astype(o_ref.dtype)


def _attn_prompt(q, k, vt, bias_t, lam_l, subln_col, bsz, seq, lam_init, t):
    nq = seq // t
    steps = [(i, j, ph) for i in range(nq) for ph in range(2) for j in range(i + 1)]
    itab = jnp.asarray([st[0] for st in steps], jnp.int32)
    jtab = jnp.asarray([st[1] for st in steps], jnp.int32)
    ptab = jnp.asarray([st[2] for st in steps], jnp.int32)

    def kind(it, jt, p):
        d = it[p] - jt[p]
        return jnp.where(d == 0, 2, jnp.where(d == 1, 1, 0))

    grid_spec = pltpu.PrefetchScalarGridSpec(
        num_scalar_prefetch=3,
        grid=(bsz, H_DA, len(steps)),
        in_specs=[pl.BlockSpec((None, t, DV_DA), lambda b, h, p, it, jt, ph: (b, it[p], h)),
                  pl.BlockSpec((None, t, DV_DA), lambda b, h, p, it, jt, ph: (b, jt[p], h)),
                  pl.BlockSpec((None, None, DV_DA, t), lambda b, h, p, it, jt, ph: (b, h, 0, jt[p] * ph[p])),
                  pl.BlockSpec((None, None, t, t), lambda b, h, p, it, jt, ph: (h, kind(it, jt, p), 0, 0)),
                  pl.BlockSpec((4, DK_DA), lambda b, h, p, it, jt, ph: (0, 0)),
                  pl.BlockSpec((DV_DA, 1), lambda b, h, p, it, jt, ph: (0, 0))],
        out_specs=pl.BlockSpec((None, t, DV_DA), lambda b, h, p, it, jt, ph: (b, it[p], h)),
        scratch_shapes=[pltpu.VMEM((t, DV_DA), BF16), pltpu.VMEM((t, DV_DA), BF16),
                        pltpu.VMEM((1, t), F32), pltpu.VMEM((1, t), F32),
                        pltpu.VMEM((1, t), F32), pltpu.VMEM((1, t), F32),
                        pltpu.VMEM((DV_DA, t), F32), pltpu.VMEM((1, LANES), F32)])
    return pl.pallas_call(
        functools.partial(_attn_kernel, lam_init=lam_init),
        grid_spec=grid_spec,
        out_shape=jax.ShapeDtypeStruct((bsz, seq, D_BRANCH), BF16),
        compiler_params=_cparams(("arbitrary", "arbitrary", "arbitrary"), 40),
        name="attn",
    )(itab, jtab, ptab, q, k, vt, bias_t, lam_l, subln_col)


def _mlstm_kernel(q_ref, k_ref, v_ref, o_ref, i_ref, f_ref, ct0_ref, n0_ref, m0_ref, nrm_ref,
                  h_ref, ct_ref, n_ref, m_ref, ct_s, n_s, m_s):
    c_idx = pl.program_id(2)

    @pl.when(c_idx == 0)
    def _init():
        ct_s[...] = ct0_ref[...]
        n_s[...] = n0_ref[...]
        m_s[...] = m0_ref[...]

    L = q_ref.shape[0]
    q = q_ref[...]
    k = k_ref[...] * (DQK_M ** -0.5)
    v = v_ref[...]
    i_row = i_ref[...]
    logf = _log_sigmoid(f_ref[...])
    r_io = lax.broadcasted_iota(jnp.int32, (L, L), 0)
    c_io = lax.broadcasted_iota(jnp.int32, (L, L), 1)
    upper = (r_io <= c_io).astype(F32)
    b_row = jnp.dot(jnp.broadcast_to(logf, (8, L)), upper, precision=HI,
                    preferred_element_type=F32)[0:1, :]
    stack = jnp.where(r_io == 0, b_row, jnp.where(r_io == 1, i_row, 0.0))
    stack_t = stack.T
    b_col = stack_t[:, 0:1]
    i_col = stack_t[:, 1:2]
    d = jnp.where(r_io >= c_io, b_col - b_row + i_row, NEG)
    m_prev = m_s[:, 0:1]
    inter = b_col + m_prev
    m_t = jnp.maximum(inter, jnp.max(d, axis=1, keepdims=True))
    w_inter = jnp.exp(inter - m_t)
    qb = q.astype(BF16)
    kb = k.astype(BF16)
    nt = (((1,), (1,)), ((), ()))
    qk = lax.dot_general(qb, kb, nt, preferred_element_type=F32) * jnp.exp(d - m_t)
    ct = ct_s[...]
    num = (jnp.dot(qk.astype(BF16), v.astype(BF16), preferred_element_type=F32)
           + w_inter * jnp.dot(qb, ct.astype(BF16), preferred_element_type=F32))
    n_row = n_s[...]
    den = jnp.sum(qk, axis=1, keepdims=True) + w_inter * jnp.sum(_rnd(q) * _rnd_bits(n_row), axis=1, keepdims=True)
    h = num / jnp.maximum(jnp.abs(den), jnp.exp(-m_t))
    m_new = m_t[L - 1:L, :]
    b_last = b_row[:, L - 1:L]
    w_end = jnp.exp(b_last - b_col + i_col - m_new)
    decay = jnp.exp(b_last + m_prev - m_new)
    ct_new = decay * ct + jnp.dot((w_end * k).T.astype(BF16), v.astype(BF16), preferred_element_type=F32)
    n_new = decay * n_row + jnp.sum(_rnd_bits(w_end) * _rnd(k), axis=0, keepdims=True)
    ct_s[...] = ct_new
    n_s[...] = n_new
    m_s[...] = jnp.broadcast_to(m_new, m_s.shape)
    ct_ref[...] = ct_new
    n_ref[...] = n_new
    m_ref[...] = jnp.broadcast_to(m_new, m_ref.shape)
    hn = h * lax.rsqrt(jnp.mean(h * h, axis=1, keepdims=True) + LN_EPS) * nrm_ref[...]
    h_ref[...] = (hn * _sigmoid(o_ref[...])).astype(h_ref.dtype)


def _mlstm_prompt(mqk, mv, mo, i_rows, f_rows, ct0, n0, m0, mnorm, bsz, seq):
    L = M_CHUNK if seq % M_CHUNK == 0 else seq
    nc = seq // L
    st = lambda b, h, c: (b, h, 0, 0)
    return pl.pallas_call(
        _mlstm_kernel,
        grid=(bsz, H_M, nc),
        in_specs=[pl.BlockSpec((None, L, DQK_M), lambda b, h, c: (b, c, h)),
                  pl.BlockSpec((None, L, DQK_M), lambda b, h, c: (b, c, H_M + h)),
                  pl.BlockSpec((None, L, DV_M), lambda b, h, c: (b, c, h)),
                  pl.BlockSpec((None, L, DV_M), lambda b, h, c: (b, c, h)),
                  pl.BlockSpec((None, None, 1, L), lambda b, h, c: (b, h, 0, c)),
                  pl.BlockSpec((None, None, 1, L), lambda b, h, c: (b, h, 0, c)),
                  pl.BlockSpec((None, None, DQK_M, DV_M), st),
                  pl.BlockSpec((None, None, 1, DQK_M), st),
                  pl.BlockSpec((None, None, 1, LANES), st),
                  pl.BlockSpec((1, DV_M), lambda b, h, c: (0, 0))],
        out_specs=[pl.BlockSpec((None, L, DV_M), lambda b, h, c: (b, c, h)),
                   pl.BlockSpec((None, None, DQK_M, DV_M), st),
                   pl.BlockSpec((None, None, 1, DQK_M), st),
                   pl.BlockSpec((None, None, 1, LANES), st)],
        out_shape=[jax.ShapeDtypeStruct((bsz, seq, D_BRANCH), BF16),
                   jax.ShapeDtypeStruct((bsz, H_M, DQK_M, DV_M), F32),
                   jax.ShapeDtypeStruct((bsz, H_M, 1, DQK_M), F32),
                   jax.ShapeDtypeStruct((bsz, H_M, 1, LANES), F32)],
        scratch_shapes=[pltpu.VMEM((DQK_M, DV_M), F32), pltpu.VMEM((1, DQK_M), F32), pltpu.VMEM((1, LANES), F32)],
        compiler_params=_cparams(("arbitrary", "arbitrary", "arbitrary"), 32),
        name="mlstm",
    )(mqk, mqk, mv, mo, i_rows, f_rows, ct0, n0, m0, mnorm)


def _merge_kernel(b0_ref, b1_ref, b2_ref, g0_ref, g1_ref, g2_ref, w_ref, o_ref, wbf_ref):
    @pl.when(pl.program_id(1) == 0)
    def _cast():
        wbf_ref[...] = w_ref[...].astype(BF16)

    acc = _sigmoid(g0_ref[...]) * jnp.dot(b0_ref[...], wbf_ref[0], preferred_element_type=F32)
    acc = acc + _sigmoid(g1_ref[...]) * jnp.dot(b1_ref[...], wbf_ref[1], preferred_element_type=F32)
    acc = acc + _sigmoid(g2_ref[...]) * jnp.dot(b2_ref[...], wbf_ref[2], preferred_element_type=F32)
    o_ref[...] = acc.astype(o_ref.dtype)


def _merge_prompt(br0, br1, br2, zg, w_branch, layer, tm=512, tn=512):
    m, kb = br0.shape
    d = zg.shape[1] // N_BRANCH
    tm = _tile(m, tm)
    tn = _tile(d, tn)
    nn = d // tn
    bspec = pl.BlockSpec((tm, kb), lambda j, i: (i, 0))
    gspec = lambda n: pl.BlockSpec((tm, tn), lambda j, i: (i, n * nn + j))
    return pl.pallas_call(
        _merge_kernel,
        grid=(nn, m // tm),
        in_specs=[bspec, bspec, bspec, gspec(0), gspec(1), gspec(2),
                  pl.BlockSpec((None, N_BRANCH, kb, tn), lambda j, i: (layer, 0, 0, j))],
        out_specs=pl.BlockSpec((tm, tn), lambda j, i: (i, j)),
        out_shape=jax.ShapeDtypeStruct((m, d), BF16),
        scratch_shapes=[pltpu.VMEM((N_BRANCH, kb, tn), BF16)],
        compiler_params=_cparams(("arbitrary", "arbitrary"), 48),
        name="merge",
    )(br0, br1, br2, zg, zg, zg, w_branch)


def _outproj_ln_kernel(a_ref, w_ref, x_ref, g_ref, b_ref, o_ref, obf_ref, acc_ref, *, alpha, tn):
    j = pl.program_id(1)
    nn = acc_ref.shape[0]
    acc_ref[j] = jnp.dot(a_ref[...], w_ref[...].astype(BF16), preferred_element_type=F32)

    @pl.when(j == nn - 1)
    def _ln():
        d = nn * tn
        ys = [alpha * x_ref[:, c * tn:(c + 1) * tn] + acc_ref[c] for c in range(nn)]
        mu = sum(jnp.sum(y, axis=-1, keepdims=True) for y in ys) / d
        ys = [y - mu for y in ys]
        var = sum(jnp.sum(y * y, axis=-1, keepdims=True) for y in ys) / d
        inv = lax.rsqrt(var + LN_EPS)
        for c in range(nn):
            sl = slice(c * tn, (c + 1) * tn)
            out = ys[c] * inv * g_ref[:, sl] + b_ref[:, sl]
            o_ref[:, sl] = out
            obf_ref[:, sl] = out.astype(BF16)


def _outproj_ln(a_bf, w, layer, x, g, b, alpha, tm=512, tn=512):
    m, k = a_bf.shape
    d = x.shape[1]
    tm = _tile(m, tm)
    tn = _tile(d, tn)
    row = lambda i, j: (i, 0)
    return pl.pallas_call(
        functools.partial(_outproj_ln_kernel, alpha=alpha, tn=tn),
        grid=(m // tm, d // tn),
        in_specs=[pl.BlockSpec((tm, k), row),
                  pl.BlockSpec((None, k, tn), lambda i, j: (layer, 0, j)),
                  pl.BlockSpec((tm, d), row),
                  pl.BlockSpec((1, d), lambda i, j: (0, 0)),
                  pl.BlockSpec((1, d), lambda i, j: (0, 0))],
        out_specs=[pl.BlockSpec((tm, d), row), pl.BlockSpec((tm, d), row)],
        out_shape=[jax.ShapeDtypeStruct((m, d), F32), jax.ShapeDtypeStruct((m, d), BF16)],
        scratch_shapes=[pltpu.VMEM((d // tn, tm, tn), F32)],
        compiler_params=_cparams(("arbitrary", "arbitrary"), 48),
        name="outproj_ln",
    )(a_bf, w, x, g, b)


def _route(z):
    lane = lax.broadcasted_iota(jnp.int32, z.shape, 1)
    lane_f = lane.astype(F32)
    big = float(LANES)
    gmask = lane < N_GROUPS
    lg = jnp.where(gmask, z, NEG)
    gmax = jnp.max(lg, axis=1, keepdims=True)
    gsel = jnp.min(jnp.where(gmask & (lg == gmax), lane_f, big), axis=1, keepdims=True)
    pg_sel = 1.0 / jnp.sum(jnp.where(gmask, jnp.exp(lg - gmax), 0.0), axis=1, keepdims=True)
    lo = N_GROUPS + EXP_PER_GROUP * gsel
    emask = (lane_f >= lo) & (lane_f < lo + EXP_PER_GROUP)
    le = jnp.where(emask, z, NEG)
    emax = jnp.max(le, axis=1, keepdims=True)
    pe = jnp.where(emask, jnp.exp(le - emax), 0.0)
    pe = pe / jnp.sum(pe, axis=1, keepdims=True)
    pm = jnp.where(emask, pe, -1.0)
    p1 = jnp.max(pm, axis=1, keepdims=True)
    i1 = jnp.min(jnp.where(pm == p1, lane_f, big), axis=1, keepdims=True)
    pm2 = jnp.where(lane_f == i1, -1.0, pm)
    p2 = jnp.max(pm2, axis=1, keepdims=True)
    i2 = jnp.min(jnp.where(pm2 == p2, lane_f, big), axis=1, keepdims=True)
    tot = p1 + p2
    w1 = pg_sel * p1 / tot
    w2 = pg_sel * p2 / tot
    wts = jnp.where(lane == 0, w1, jnp.where(lane == 1, w2, 0.0))
    ids = jnp.where(lane == 0, i1 - N_GROUPS, jnp.where(lane == 1, i2 - N_GROUPS, 0.0)).astype(jnp.int32)
    return wts, ids


def _router_kernel(x_ref, w_ref, b_ref, wt_ref, id_ref):
    z = jnp.dot(x_ref[...], w_ref[...].astype(BF16), preferred_element_type=F32) + b_ref[...]
    wts, ids = _route(z)
    wt_ref[...] = wts
    id_ref[...] = ids


def _router(x, w_r, b_r, tm=256):
    m, k = x.shape
    tm = _tile(m, tm)
    return pl.pallas_call(
        _router_kernel,
        grid=(m // tm,),
        in_specs=[pl.BlockSpec((tm, k), lambda i: (i, 0)),
                  pl.BlockSpec((k, LANES), lambda i: (0, 0)),
                  pl.BlockSpec((1, LANES), lambda i: (0, 0))],
        out_specs=[pl.BlockSpec((tm, LANES), lambda i: (i, 0)), pl.BlockSpec((tm, LANES), lambda i: (i, 0))],
        out_shape=[jax.ShapeDtypeStruct((m, LANES), F32), jax.ShapeDtypeStruct((m, LANES), jnp.int32)],
        compiler_params=_cparams(("arbitrary",), 32),
        name="router",
    )(x, w_r, b_r)


def _router_weights(w_rg, b_rg, w_re, b_re):
    k = w_rg.shape[0]
    w = jnp.zeros((k, LANES), F32).at[:, :N_GROUPS].set(w_rg).at[:, N_GROUPS:N_GROUPS + N_EXPERTS].set(w_re)
    b = jnp.zeros((1, LANES), F32).at[0, :N_GROUPS].set(b_rg).at[0, N_GROUPS:N_GROUPS + N_EXPERTS].set(b_re)
    return w, b


def _row_copy(src_hbm, row, dst_ref, r, sem):
    return pltpu.make_async_copy(src_hbm.at[pl.ds(row, 1)], dst_ref.at[pl.ds(r, 1)], sem)


def _gather_kernel(src_ref, x_hbm, o_ref, sem):
    i = pl.program_id(0)
    te = o_ref.shape[0]

    def issue(r, c):
        _row_copy(x_hbm, src_ref[i * te + r], o_ref, r, sem).start()
        return c

    lax.fori_loop(0, te, issue, 0)

    def wait(r, c):
        _row_copy(x_hbm, 0, o_ref, r, sem).wait()
        return c

    lax.fori_loop(0, te, wait, 0)


def _gather_rows(x, row_src, te):
    nr = row_src.shape[0]
    d = x.shape[1]
    grid_spec = pltpu.PrefetchScalarGridSpec(
        num_scalar_prefetch=1,
        grid=(nr // te,),
        in_specs=[pl.BlockSpec(memory_space=pl.ANY)],
        out_specs=pl.BlockSpec((te, d), lambda i, s: (i, 0)),
        scratch_shapes=[pltpu.SemaphoreType.DMA(())])
    return pl.pallas_call(
        _gather_kernel,
        grid_spec=grid_spec,
        out_shape=jax.ShapeDtypeStruct((nr, d), x.dtype),
        compiler_params=_cparams(("arbitrary",), 32),
        name="moe_gather",
    )(row_src, x)


def _expert_kernel(te_ref, first_ref, nused_ref, xs_ref, rw_ref, wg_ref, wu_ref, wd_ref, y_ref,
                   wg_bf, wu_bf, wd_bf):
    i = pl.program_id(0)

    @pl.when(first_ref[i] == 1)
    def _cast():
        wg_bf[...] = wg_ref[...].astype(BF16)
        wu_bf[...] = wu_ref[...].astype(BF16)
        wd_bf[...] = wd_ref[...].astype(BF16)

    @pl.when(i < nused_ref[0])
    def _compute():
        xb = xs_ref[...].astype(BF16)
        g = jnp.dot(xb, wg_bf[...], preferred_element_type=F32)
        u = jnp.dot(xb, wu_bf[...], preferred_element_type=F32)
        hid = g * _sigmoid(g) * u * rw_ref[...]
        y_ref[...] = jnp.dot(hid.astype(BF16), wd_bf[...], preferred_element_type=F32)

    @pl.when(i >= nused_ref[0])
    def _pad():
        y_ref[...] = jnp.zeros(y_ref.shape, y_ref.dtype)


def _expert_mlp(xs, row_w, tile_expert, tile_first, n_used, w_gate, w_up, w_down, layer, te):
    nr, d = xs.shape
    f = w_gate.shape[-1]
    grid_spec = pltpu.PrefetchScalarGridSpec(
        num_scalar_prefetch=3,
        grid=(nr // te,),
        in_specs=[pl.BlockSpec((te, d), lambda i, e, fi, nu: (i, 0)),
                  pl.BlockSpec((te, 1), lambda i, e, fi, nu: (i, 0)),
                  pl.BlockSpec((None, None, d, f), lambda i, e, fi, nu: (layer, e[i], 0, 0)),
                  pl.BlockSpec((None, None, d, f), lambda i, e, fi, nu: (layer, e[i], 0, 0)),
                  pl.BlockSpec((None, None, f, d), lambda i, e, fi, nu: (layer, e[i], 0, 0))],
        out_specs=pl.BlockSpec((te, d), lambda i, e, fi, nu: (i, 0)),
        scratch_shapes=[pltpu.VMEM((d, f), BF16), pltpu.VMEM((d, f), BF16), pltpu.VMEM((f, d), BF16)])
    return pl.pallas_call(
        _expert_kernel,
        grid_spec=grid_spec,
        out_shape=jax.ShapeDtypeStruct((nr, d), F32),
        compiler_params=_cparams(("arbitrary",), 52),
        name="moe_experts",
    )(tile_expert, tile_first, n_used, xs, row_w, w_gate, w_up, w_down)


def _combine_ln_kernel(pos_ref, x_ref, ys_hbm, g_ref, b_ref, o_ref, obf_ref, buf0, buf1, sem, *, alpha):
    i = pl.program_id(0)
    tm = x_ref.shape[0]

    def issue(r, c):
        t = i * tm + r
        _row_copy(ys_hbm, pos_ref[2 * t], buf0, r, sem.at[0]).start()
        _row_copy(ys_hbm, pos_ref[2 * t + 1], buf1, r, sem.at[1]).start()
        return c

    lax.fori_loop(0, tm, issue, 0)

    def wait(r, c):
        _row_copy(ys_hbm, 0, buf0, r, sem.at[0]).wait()
        _row_copy(ys_hbm, 0, buf1, r, sem.at[1]).wait()
        return c

    lax.fori_loop(0, tm, wait, 0)
    y = alpha * x_ref[...] + (buf0[...] + buf1[...])
    out = _layer_norm_rows(y, g_ref[...], b_ref[...])
    o_ref[...] = out
    obf_ref[...] = out.astype(BF16)


def _combine_ln(x, ys, pos, g, b, alpha, tm=256):
    m, d = x.shape
    tm = _tile(m, tm)
    grid_spec = pltpu.PrefetchScalarGridSpec(
        num_scalar_prefetch=1,
        grid=(m // tm,),
        in_specs=[pl.BlockSpec((tm, d), lambda i, p: (i, 0)),
                  pl.BlockSpec(memory_space=pl.ANY),
                  pl.BlockSpec((1, d), lambda i, p: (0, 0)),
                  pl.BlockSpec((1, d), lambda i, p: (0, 0))],
        out_specs=[pl.BlockSpec((tm, d), lambda i, p: (i, 0)), pl.BlockSpec((tm, d), lambda i, p: (i, 0))],
        scratch_shapes=[pltpu.VMEM((tm, d), F32), pltpu.VMEM((tm, d), F32), pltpu.SemaphoreType.DMA((2,))])
    return pl.pallas_call(
        functools.partial(_combine_ln_kernel, alpha=alpha),
        grid_spec=grid_spec,
        out_shape=[jax.ShapeDtypeStruct((m, d), F32), jax.ShapeDtypeStruct((m, d), BF16)],
        compiler_params=_cparams(("arbitrary",), 40),
        name="moe_combine_ln",
    )(pos, x, ys, g, b)


def _dispatch_plan(ids, wts, te):
    t = ids.shape[0]
    a = 2 * t
    nr = a + N_EXPERTS * te
    e_flat = ids.reshape(a)
    w_flat = wts.reshape(a)
    order = jnp.argsort(e_flat, stable=True).astype(jnp.int32)
    e_sorted = e_flat[order]
    counts = jnp.sum(jax.nn.one_hot(e_flat, N_EXPERTS, dtype=jnp.int32), axis=0)
    padded = ((counts + te - 1) // te) * te
    pad_end = jnp.cumsum(padded)
    pad_off = pad_end - padded
    off = jnp.cumsum(counts) - counts
    dest = (pad_off[e_sorted] + jnp.arange(a, dtype=jnp.int32) - off[e_sorted]).astype(jnp.int32)
    row_src = jnp.zeros((nr,), jnp.int32).at[dest].set(order // 2)
    row_w = jnp.zeros((nr,), F32).at[dest].set(w_flat[order]).reshape(nr, 1)
    pos = jnp.zeros((a,), jnp.int32).at[order].set(dest)
    n_used = (pad_end[-1] // te).astype(jnp.int32)
    tile_start = jnp.arange(nr // te, dtype=jnp.int32) * te
    tile_expert = jnp.searchsorted(pad_end, tile_start, side="right").astype(jnp.int32)
    last_e = jnp.max(e_flat).astype(jnp.int32)
    tile_expert = jnp.minimum(tile_expert, last_e)
    prev = jnp.concatenate([jnp.full((1,), -1, jnp.int32), tile_expert[:-1]])
    tile_first = (tile_expert != prev).astype(jnp.int32)
    return row_src, row_w, pos, tile_expert, tile_first, n_used.reshape(1)


def _small_mm_kernel(x_ref, w_ref, b_ref, o_ref):
    o_ref[...] = jnp.dot(x_ref[...].astype(BF16), w_ref[...].astype(BF16), preferred_element_type=F32) + b_ref[...]


def _small_mm_t_kernel(x_ref, w_ref, b_ref, o_ref):
    o_ref[...] = lax.dot_general(x_ref[...].astype(BF16), w_ref[...].astype(BF16), (((1,), (1,)), ((), ())),
                                 preferred_element_type=F32) + b_ref[...]


def _small_mm(x, w, lead, bias=None, tn=1024, transposed=False):
    r, k = x.shape
    n = w.shape[-2] if transposed else w.shape[-1]
    tn = min(tn, n)
    nlead = len(lead)
    if bias is None:
        bias = jnp.zeros((1, n), F32)
    if transposed:
        w_spec = pl.BlockSpec((None,) * nlead + (tn, k), lambda j: tuple(lead) + (j, 0))
    else:
        w_spec = pl.BlockSpec((None,) * nlead + (k, tn), lambda j: tuple(lead) + (0, j))
    return pl.pallas_call(
        _small_mm_t_kernel if transposed else _small_mm_kernel,
        grid=(pl.cdiv(n, tn),),
        in_specs=[pl.BlockSpec((r, k), lambda j: (0, 0)),
                  w_spec,
                  pl.BlockSpec((1, tn), lambda j: (0, j))],
        out_specs=pl.BlockSpec((r, tn), lambda j: (0, j)),
        out_shape=jax.ShapeDtypeStruct((r, n), F32),
        compiler_params=_cparams(("arbitrary",), 40),
        name="small_mm",
    )(x, w, bias)


def _small_call(fn, out_shapes, *args, name="small"):
    n_in = len(args)

    def kern(*refs):
        outs = fn(*[r[...] for r in refs[:n_in]])
        for r, o in zip(refs[n_in:], outs):
            r[...] = o

    return pl.pallas_call(kern, out_shape=out_shapes, name=name)(*args)


def _sample_conv_fn(zb, zc, zx, buf0, buf1, w):
    u = zc * zx
    y = buf0 * w[0:1, :]
    y = y + buf1 * w[1:2, :]
    y = y + u * w[2:3, :]
    return zb * y, u


def _sample_mlstm_kernel(q_ref, k_ref, v_ref, vc_ref, o_ref, i_ref, f_ref, c_ref, n_ref, m_ref, nrm_ref,
                         h_ref, c_out, n_out, m_out):
    for hh in range(H_M):
        q = q_ref[hh]
        k = k_ref[hh] * (DQK_M ** -0.5)
        v = v_ref[hh]
        v_col = vc_ref[hh]
        ig = i_ref[hh][:, 0:1]
        logf = _log_sigmoid(f_ref[hh][:, 0:1])
        m_prev = m_ref[hh][:, 0:1]
        c = c_ref[hh]
        n = n_ref[hh]
        inter = logf + m_prev
        m_t = jnp.maximum(inter, ig)
        w_inter = jnp.exp(inter - m_t)
        qk = jnp.sum(q * k, axis=1, keepdims=True) * jnp.exp(ig - m_t)
        cq = lax.dot_general(jnp.broadcast_to(q, (8, DQK_M)).astype(BF16), c.astype(BF16),
                             (((1,), (1,)), ((), ())), preferred_element_type=F32)[0:1, :]
        num = qk * v + w_inter * cq
        den = qk + w_inter * jnp.sum(n * q, axis=1, keepdims=True)
        h = num / jnp.maximum(jnp.abs(den), jnp.exp(-m_t))
        w_end = jnp.exp(ig - m_t)
        decay = jnp.exp(logf + m_prev - m_t)
        c_out[hh] = decay * c + w_end * (v_col * k)
        n_out[hh] = decay * n + w_end * k
        m_out[hh] = jnp.broadcast_to(m_t, (1, LANES))
        hn = h * lax.rsqrt(jnp.mean(h * h, axis=1, keepdims=True) + LN_EPS) * nrm_ref[...]
        h_ref[hh] = hn * _sigmoid(o_ref[hh])


def _sample_mlstm(mq, mk, mv, mo, mi, mf, c0, n0, m0, mnorm):
    r = mq.shape[0]
    mv_col = mv.reshape(r, H_M, DV_M, 1)
    blk = lambda *s: pl.BlockSpec((None,) + s, lambda b: (b,) + (0,) * len(s))
    return pl.pallas_call(
        _sample_mlstm_kernel,
        grid=(r,),
        in_specs=[blk(H_M, 1, DQK_M), blk(H_M, 1, DQK_M), blk(H_M, 1, DV_M), blk(H_M, DV_M, 1), blk(H_M, 1, DV_M),
                  blk(H_M, 1, LANES), blk(H_M, 1, LANES), blk(H_M, DV_M, DQK_M), blk(H_M, 1, DQK_M),
                  blk(H_M, 1, LANES), pl.BlockSpec((1, DV_M), lambda b: (0, 0))],
        out_specs=[blk(H_M, 1, DV_M), blk(H_M, DV_M, DQK_M), blk(H_M, 1, DQK_M), blk(H_M, 1, LANES)],
        out_shape=[jax.ShapeDtypeStruct((r, H_M, 1, DV_M), F32),
                   jax.ShapeDtypeStruct((r, H_M, DV_M, DQK_M), F32),
                   jax.ShapeDtypeStruct((r, H_M, 1, DQK_M), F32),
                   jax.ShapeDtypeStruct((r, H_M, 1, LANES), F32)],
        compiler_params=_cparams(("arbitrary",), 32),
        name="sample_mlstm",
    )(mq, mk, mv, mv_col, mo, mi, mf, c0, n0, m0, mnorm)


def _class_reduce(x, op):
    sh = H_DA
    while sh < x.shape[-1]:
        x = op(x, pltpu.roll(x, sh, x.ndim - 1))
        sh *= 2
    return x


def _paged_attn_kernel(pt_ref, q_ref, kn_ref, vn_ref, k_ref, v_ref, brow_ref, bself_ref, lam_ref, sub_ref, o_ref,
                       qsel, sc, m_run, l_run, mb, ilb, acc, lam_s, *, lam_init):
    sweep = pl.program_id(1)
    p = pl.program_id(2)
    npg = pl.num_programs(2)
    rows = k_ref.shape[0]
    nt = (((1,), (1,)), ((), ()))
    r_io = lax.broadcasted_iota(jnp.int32, (2 * H_DA, rows), 0)
    c_io = lax.broadcasted_iota(jnp.int32, (2 * H_DA, rows), 1)
    own = (c_io % H_DA) == (r_io % H_DA)
    lane = lax.broadcasted_iota(jnp.int32, (H_DA, DV_DA), 1)
    first = lane < DK_DA

    def to_lanes(col):
        return jnp.sum(jnp.where(own[0:H_DA], col, 0.0), axis=0, keepdims=True)

    def to_col(row):
        return jnp.max(jnp.where(own[0:H_DA], row, NEG), axis=1, keepdims=True)

    def self_scores():
        prod = _rnd_bits(kn_ref[...]) * _rnd_bits(q_ref[...] * (DK_DA ** -0.5))
        t1 = jnp.sum(jnp.where(first, prod, 0.0), axis=1, keepdims=True) + bself_ref[...]
        t2 = jnp.sum(jnp.where(first, 0.0, prod), axis=1, keepdims=True) + bself_ref[...]
        return t1, t2

    @pl.when((sweep == 0) & (p == 0))
    def _init():
        q = q_ref[...] * (DK_DA ** -0.5)
        qsel[...] = jnp.concatenate([jnp.where(first, q, 0.0), jnp.where(first, 0.0, q)], axis=0).astype(BF16)
        m_run[...] = jnp.full(m_run.shape, NEG, F32)
        l_run[...] = jnp.zeros(l_run.shape, F32)
        lv = lam_ref[...]
        lam = (jnp.exp(jnp.sum(lv[0:1, :] * lv[1:2, :], axis=1, keepdims=True))
               - jnp.exp(jnp.sum(lv[2:3, :] * lv[3:4, :], axis=1, keepdims=True)) + lam_init)
        lam_s[...] = jnp.broadcast_to(lam, lam_s.shape)

    @pl.when(sweep == 0)
    def _scores():
        r = lax.dot_general(qsel[...], k_ref[...].astype(BF16), nt, preferred_element_type=F32)
        r = jnp.where(own, r, 0.0)
        bias = jnp.where(p == npg - 1, brow_ref[1:2, :], brow_ref[0:1, :])
        for c in range(2):
            s = jnp.sum(r[c * H_DA:(c + 1) * H_DA], axis=0, keepdims=True) + bias
            sc[p, c:c + 1, :] = s
            m_old = m_run[c:c + 1, :]
            m_new = jnp.maximum(m_old, s)
            l_run[c:c + 1, :] = l_run[c:c + 1, :] * jnp.exp(m_old - m_new) + jnp.exp(s - m_new)
            m_run[c:c + 1, :] = m_new

        @pl.when(p == npg - 1)
        def _close():
            for c, t in enumerate(self_scores()):
                tb = to_lanes(t)
                m_c = m_run[c:c + 1, :]
                m_all = jnp.maximum(_class_reduce(m_c, jnp.maximum), tb)
                l_all = _class_reduce(l_run[c:c + 1, :] * jnp.exp(m_c - m_all), jnp.add) + jnp.exp(tb - m_all)
                mb[c:c + 1, :] = m_all
                ilb[c:c + 1, :] = 1.0 / l_all
            acc[...] = jnp.zeros(acc.shape, F32)

    @pl.when(sweep == 1)
    def _apply():
        lam = lam_s[:, 0:1]
        w1 = jnp.exp(sc[p, 0:1, :] - mb[0:1, :]) * ilb[0:1, :]
        w2 = jnp.exp(sc[p, 1:2, :] - mb[1:2, :]) * ilb[1:2, :]
        a = jnp.where(own, w1 - lam * w2, 0.0).astype(BF16)
        acc[...] += jnp.dot(a, v_ref[...].astype(BF16), preferred_element_type=F32)

        @pl.when(p == npg - 1)
        def _fin():
            t1, t2 = self_scores()
            w1s = jnp.exp(t1 - to_col(mb[0:1, :])) * to_col(ilb[0:1, :])
            w2s = jnp.exp(t2 - to_col(mb[1:2, :])) * to_col(ilb[1:2, :])
            o = acc[0:H_DA, :] + _rnd_bits(w1s - lam * w2s) * _rnd_bits(vn_ref[...])
            on = o * lax.rsqrt(jnp.mean(o * o, axis=1, keepdims=True) + LN_EPS) * sub_ref[...] * (1.0 - lam_init)
            o_ref[...] = on


def _paged_attn(q, k_new, v_new, cache_k, cache_v, page_table, layer, bias_rows, bias_self, lam_l, subln_row,
                lam_init):
    r = q.shape[0]
    npg = page_table.shape[1]
    psz = cache_k.shape[2]
    rows = psz * H_DA
    pt = page_table.reshape(-1)
    tok = lambda b, s, p, t: (b, 0, 0)
    cst2 = lambda b, s, p, t: (0, 0)
    kpage = lambda b, s, p, t: (layer, t[b * npg + p * (1 - s) + (npg - 1) * s], 0, 0)
    vpage = lambda b, s, p, t: (layer, t[b * npg + p * s], 0, 0)
    pool = cache_k.shape[1]
    cache_k = cache_k.reshape(cache_k.shape[0], pool, rows, DV_DA)
    cache_v = cache_v.reshape(cache_v.shape[0], pool, rows, DV_DA)
    grid_spec = pltpu.PrefetchScalarGridSpec(
        num_scalar_prefetch=1,
        grid=(r, 2, npg),
        in_specs=[pl.BlockSpec((None, H_DA, DV_DA), tok),
                  pl.BlockSpec((None, H_DA, DV_DA), tok),
                  pl.BlockSpec((None, H_DA, DV_DA), tok),
                  pl.BlockSpec((None, None, rows, DV_DA), kpage),
                  pl.BlockSpec((None, None, rows, DV_DA), vpage),
                  pl.BlockSpec((2, rows), cst2),
                  pl.BlockSpec((H_DA, 1), cst2),
                  pl.BlockSpec((4, DK_DA), cst2),
                  pl.BlockSpec((1, DV_DA), cst2)],
        out_specs=pl.BlockSpec((None, H_DA, DV_DA), tok),
        scratch_shapes=[pltpu.VMEM((2 * H_DA, DV_DA), BF16), pltpu.VMEM((npg, 2, rows), F32),
                        pltpu.VMEM((2, rows), F32), pltpu.VMEM((2, rows), F32),
                        pltpu.VMEM((2, rows), F32), pltpu.VMEM((2, rows), F32),
                        pltpu.VMEM((2 * H_DA, DV_DA), F32), pltpu.VMEM((1, LANES), F32)])
    return pl.pallas_call(
        functools.partial(_paged_attn_kernel, lam_init=lam_init),
        grid_spec=grid_spec,
        out_shape=jax.ShapeDtypeStruct((r, H_DA, DV_DA), F32),
        compiler_params=_cparams(("arbitrary", "arbitrary", "arbitrary"), 32),
        name="paged_attn",
    )(pt, q, k_new, v_new, cache_k, cache_v, bias_rows, bias_self, lam_l, subln_row)


def _sample_moe_kernel(x_ref, cmb_ref, wg_ref, wu_ref, wd_ref, y_ref):
    e = pl.program_id(0)

    @pl.when(e == 0)
    def _init():
        y_ref[...] = jnp.zeros(y_ref.shape, F32)

    x = x_ref[...].astype(BF16)
    g = jnp.dot(x, wg_ref[...].astype(BF16), preferred_element_type=F32)
    u = jnp.dot(x, wu_ref[...].astype(BF16), preferred_element_type=F32)
    hid = g * _sigmoid(g) * u * cmb_ref[...]
    y_ref[...] += jnp.dot(hid.astype(BF16), wd_ref[...].astype(BF16), preferred_element_type=F32)


def _sample_moe(x, combine_t, w_gate, w_up, w_down, layer):
    r, d = x.shape
    f = w_gate.shape[-1]
    return pl.pallas_call(
        _sample_moe_kernel,
        grid=(N_EXPERTS,),
        in_specs=[pl.BlockSpec((r, d), lambda e: (0, 0)),
                  pl.BlockSpec((None, r, 1), lambda e: (e, 0, 0)),
                  pl.BlockSpec((None, None, d, f), lambda e: (layer, e, 0, 0)),
                  pl.BlockSpec((None, None, d, f), lambda e: (layer, e, 0, 0)),
                  pl.BlockSpec((None, None, f, d), lambda e: (layer, e, 0, 0))],
        out_specs=pl.BlockSpec((r, d), lambda e: (0, 0)),
        out_shape=jax.ShapeDtypeStruct((r, d), F32),
        compiler_params=_cparams(("arbitrary",), 48),
        name="sample_moe",
    )(x, combine_t, w_gate, w_up, w_down)


def _prompt_layer(l, x, x_bf, bsz, seq, bias_t, attn_t, moe_te, p):
    f32 = F32
    alpha = (2.0 * p["depth"]) ** 0.25
    lam_init = 0.8 - 0.6 * math.exp(-0.3 * l)
    t = bsz * seq
    w_in_t, b_in = p["w_in_t"], p["b_in"]
    bseg = lambda o, n: b_in[l, o:o + n].reshape(1, n)
    pr = lambda o, n, dt=f32: _proj(x_bf, w_in_t, l, o, n, bseg(o, n), dt)
    z_conv = pr(OFF_CONV, 3 * D_BRANCH)
    a_q = pr(OFF_AQ, D_BRANCH)
    a_k = pr(OFF_AK, D_BRANCH)
    a_v = pr(OFF_AV, D_BRANCH)
    m_qk = pr(OFF_MQ, 2 * H_M * DQK_M)
    m_v = pr(OFF_MV, D_BRANCH)
    m_o = pr(OFF_MO, D_BRANCH)
    z_g = pr(OFF_G, N_BRANCH * x.shape[1])
    z_if = pr(OFF_MI, LANES)

    y_conv, conv_buf = _conv_prompt(z_conv, p["conv_w"][l], jnp.zeros((bsz, CONV_W - 1, D_BRANCH), f32), bsz, seq)

    vt = a_v.reshape(bsz, seq, H_DA, DV_DA).transpose(0, 2, 3, 1).astype(BF16)
    o = _attn_prompt(a_q.reshape(bsz, seq, D_BRANCH), a_k.reshape(bsz, seq, D_BRANCH), vt, bias_t,
                     p["attn_lambda"][l], p["attn_subln"][l].reshape(DV_DA, 1), bsz, seq, lam_init, attn_t)
    o = o.reshape(t, D_BRANCH)

    gates = z_if[:, :2 * H_M].reshape(bsz, seq, 2, H_M).transpose(2, 0, 3, 1).reshape(2, bsz, H_M, 1, seq)
    h, ct1, n1, m1 = _mlstm_prompt(
        m_qk.reshape(bsz, seq, D_BRANCH), m_v.reshape(bsz, seq, D_BRANCH), m_o.reshape(bsz, seq, D_BRANCH),
        gates[0], gates[1],
        jnp.zeros((bsz, H_M, DQK_M, DV_M), f32), jnp.zeros((bsz, H_M, 1, DQK_M), f32),
        jnp.zeros((bsz, H_M, 1, LANES), f32), p["mlstm_norm"][l].reshape(1, DV_M), bsz, seq)
    h = h.reshape(t, D_BRANCH)

    merged = _merge_prompt(y_conv, o, h, z_g, p["w_branch"], l)
    x1, x1_bf = _outproj_ln(merged, p["w_out"], l, x, p["ln_g"][l, 0].reshape(1, -1),
                            p["ln_b"][l, 0].reshape(1, -1), alpha)

    w_r, b_r = p["router"][l]
    wts, ids = _router(x1_bf, w_r, b_r)
    row_src, row_w, pos, tile_e, tile_first, n_used = _dispatch_plan(ids[:, :2], wts[:, :2], moe_te)
    xs = _gather_rows(x1, row_src, moe_te)
    ys = _expert_mlp(xs, row_w, tile_e, tile_first, n_used, p["w_gate"], p["w_up"], p["w_down"], l, moe_te)
    x2, x2_bf = _combine_ln(x1, ys, pos, p["ln_g"][l, 1].reshape(1, -1), p["ln_b"][l, 1].reshape(1, -1), alpha)

    state = (a_k.reshape(bsz, seq, H_DA, 2 * DK_DA), a_v.reshape(bsz, seq, H_DA, DV_DA), conv_buf,
             jnp.swapaxes(ct1, -1, -2), n1.reshape(bsz, H_M, DQK_M), m1[:, :, 0, 0])
    return x2, x2_bf, state


def _sample_layer(l, x, conv_state, c0, n0, m0, cache_k, cache_v, page_table, dec_bias, p):
    f32 = F32
    alpha = (2.0 * p["depth"]) ** 0.25
    lam_init = 0.8 - 0.6 * math.exp(-0.3 * l)
    r, d = x.shape
    z = _small_mm(x, p["w_in_t"], (l,), p["b_in"][l].reshape(1, -1), transposed=True)
    seg = lambda o, n: z[:, o:o + n]
    y_conv, u = _small_call(
        _sample_conv_fn, [jax.ShapeDtypeStruct((r, D_BRANCH), f32)] * 2,
        seg(OFF_CONV, D_BRANCH), seg(OFF_CONV + D_BRANCH, D_BRANCH), seg(OFF_CONV + 2 * D_BRANCH, D_BRANCH),
        conv_state[:, 0], conv_state[:, 1], p["conv_w"][l], name="sample_conv")
    new_buf = jnp.stack([conv_state[:, 1], u], axis=1)

    a_q = seg(OFF_AQ, D_BRANCH).reshape(r, H_DA, DV_DA)
    a_k = seg(OFF_AK, D_BRANCH).reshape(r, H_DA, DV_DA)
    a_v = seg(OFF_AV, D_BRANCH).reshape(r, H_DA, DV_DA)
    bias_rows, bias_self = dec_bias
    o = _paged_attn(a_q, a_k, a_v, cache_k, cache_v, page_table, l, bias_rows, bias_self,
                    p["attn_lambda"][l], p["attn_subln"][l].reshape(1, DV_DA), lam_init).reshape(r, D_BRANCH)

    lanes = lambda a: jnp.broadcast_to(a.reshape(r, H_M, 1, 1), (r, H_M, 1, LANES))
    h, c1, n1, m1 = _sample_mlstm(
        seg(OFF_MQ, H_M * DQK_M).reshape(r, H_M, 1, DQK_M), seg(OFF_MK, H_M * DQK_M).reshape(r, H_M, 1, DQK_M),
        seg(OFF_MV, D_BRANCH).reshape(r, H_M, 1, DV_M), seg(OFF_MO, D_BRANCH).reshape(r, H_M, 1, DV_M),
        lanes(seg(OFF_MI, H_M)), lanes(seg(OFF_MF, H_M)),
        c0, n0.reshape(r, H_M, 1, DQK_M), lanes(m0), p["mlstm_norm"][l].reshape(1, DV_M))
    h = h.reshape(r, D_BRANCH)

    projs = [_small_mm(br, p["w_branch"], (l, n)) for n, br in enumerate((y_conv, o, h))]
    g = seg(OFF_G, N_BRANCH * d)

    def merge_fn(g0, g1, g2, p0, p1, p2):
        return (_sigmoid(g0) * p0 + _sigmoid(g1) * p1 + _sigmoid(g2) * p2,)

    merged, = _small_call(merge_fn, [jax.ShapeDtypeStruct((r, d), f32)],
                          g[:, :d], g[:, d:2 * d], g[:, 2 * d:], *projs, name="sample_merge")
    mix = _small_mm(merged, p["w_out"], (l,))

    def ln_fn(xr, yr, gg, bb):
        return (_layer_norm_rows(alpha * xr + yr, gg, bb),)

    x1, = _small_call(ln_fn, [jax.ShapeDtypeStruct((r, d), f32)], x, mix,
                      p["ln_g"][l, 0].reshape(1, -1), p["ln_b"][l, 0].reshape(1, -1), name="sample_ln")
    w_r, b_r = p["router"][l]
    zr = _small_mm(x1, w_r, (), b_r, tn=LANES)
    wts, ids = _small_call(_route, [jax.ShapeDtypeStruct((r, LANES), f32), jax.ShapeDtypeStruct((r, LANES), jnp.int32)],
                           zr, name="sample_route")
    combine = jnp.einsum("tk,tke->te", wts[:, :2], jax.nn.one_hot(ids[:, :2], N_EXPERTS, dtype=f32))
    y = _sample_moe(x1, combine.T.reshape(N_EXPERTS, r, 1), p["w_gate"], p["w_up"], p["w_down"], l)
    x2, = _small_call(ln_fn, [jax.ShapeDtypeStruct((r, d), f32)], x1, y,
                      p["ln_g"][l, 1].reshape(1, -1), p["ln_b"][l, 1].reshape(1, -1), name="sample_ln")
    state = (a_k.reshape(r, 1, H_DA, 2 * DK_DA), a_v.reshape(r, 1, H_DA, DV_DA), new_buf,
             c1, n1.reshape(r, H_M, DQK_M), m1[:, :, 0, 0])
    return x2, state


def kernel(x_prompt, x_sample, cache_k, cache_v, state_conv, state_mlstm_c, state_mlstm_n, state_mlstm_m,
           page_table, rel_bias, w_in, b_in, conv_w, attn_lambda, attn_subln, mlstm_norm, w_branch, w_out,
           ln_g, ln_b, w_router_group, b_router_group, w_router_expert, b_router_expert, w_gate, w_up, w_down):
    depth = w_in.shape[0]
    bp, sp, d = x_prompt.shape
    bs, ss, _ = x_sample.shape
    assert ss == 1, "the sample group decodes one token per sequence"
    psz = cache_k.shape[2]
    past_len = page_table.shape[1] * psz
    assert psz >= MAX_DISTANCE, "only the last page may hold keys closer than MAX_DISTANCE"

    params = dict(
        depth=depth, b_in=b_in, conv_w=conv_w, attn_lambda=attn_lambda, attn_subln=attn_subln,
        mlstm_norm=mlstm_norm, w_branch=w_branch, w_out=w_out, ln_g=ln_g, ln_b=ln_b,
        w_gate=w_gate, w_up=w_up, w_down=w_down,
        w_in_t=jnp.swapaxes(w_in, 1, 2),
        router=[_router_weights(w_router_group[l], b_router_group[l], w_router_expert[l], b_router_expert[l])
                for l in range(depth)],
    )

    attn_t = _tile(sp, 512)
    kk = jnp.arange(attn_t, dtype=jnp.int32)[:, None]
    qq = jnp.arange(attn_t, dtype=jnp.int32)[None, :]
    far = jnp.full((attn_t, attn_t), N_BUCKETS - 1, jnp.int32)
    assert attn_t >= MAX_DISTANCE
    sub = _t5_bucket(qq + attn_t - kk)
    diag = jnp.where(qq >= kk, _t5_bucket(qq - kk), -1)
    bias_t = _bias_tiles(rel_bias, jnp.stack([far, sub, diag]))

    dec_bias = _decode_bias(rel_bias, psz, past_len)

    xp = x_prompt.reshape(bp * sp, d)
    xp_bf = xp.astype(BF16)
    xs = x_sample.reshape(bs, d)
    st_p, st_s = [], []
    for l in range(depth):
        xp, xp_bf, st = _prompt_layer(l, xp, xp_bf, bp, sp, bias_t, attn_t, 256, params)
        st_p.append(st)
        xs, st = _sample_layer(l, xs, state_conv[l], state_mlstm_c[l], state_mlstm_n[l], state_mlstm_m[l],
                               cache_k, cache_v, page_table, dec_bias, params)
        st_s.append(st)

    stack = lambda sts, i: jnp.stack([s[i] for s in sts])
    return (xp.reshape(bp, sp, d), xs.reshape(bs, ss, d),
            stack(st_p, 0), stack(st_p, 1), stack(st_s, 0), stack(st_s, 1),
            stack(st_p, 2), stack(st_s, 2),
            stack(st_p, 3), stack(st_s, 3),
            stack(st_p, 4), stack(st_s, 4),
            stack(st_p, 5), stack(st_s, 5))


def _decode_bias(rel_bias, psz, past_len):
    row = jnp.arange(psz * H_DA, dtype=jnp.int32)
    bk_last = _t5_bucket(past_len - (past_len - psz + row // H_DA))
    bk = jnp.stack([jnp.full_like(row, N_BUCKETS - 1), bk_last, jnp.zeros_like(row)])
    dec = _bias_tiles(rel_bias, bk.reshape(3, 1, psz * H_DA))[:, :, 0, :]
    own_head = (row % H_DA)[None, :] == jnp.arange(H_DA, dtype=jnp.int32)[:, None]
    dec_rows = jnp.sum(jnp.where(own_head[:, None, :], dec, 0.0), axis=0)
    return dec_rows[0:2], dec[:, 2, 0:1]
```

```python
import functools
import math

import jax
import jax.numpy as jnp
import numpy as np
from jax import lax
from jax.experimental import pallas as pl
from jax.experimental.pallas import tpu as pltpu

D_BRANCH = 1024
N_BRANCH = 3
CONV_W = 3
DK_DA = 64
DV_DA = 2 * DK_DA
H_DA = D_BRANCH // DV_DA
DQK_M = 128
DV_M = 256
H_M = D_BRANCH // DV_M
N_BUCKETS = 32
MAX_DISTANCE = 128
N_GROUPS = 4
EXP_PER_GROUP = 4
N_EXPERTS = N_GROUPS * EXP_PER_GROUP
D_EXPERT = 512
M_CHUNK = 128
LN_EPS = 1e-5

OFF_CONV = 0
OFF_AQ = 3 * D_BRANCH
OFF_AK = OFF_AQ + D_BRANCH
OFF_AV = OFF_AK + D_BRANCH
OFF_MQ = OFF_AV + D_BRANCH
OFF_MK = OFF_MQ + H_M * DQK_M
OFF_MV = OFF_MK + H_M * DQK_M
OFF_MO = OFF_MV + D_BRANCH
OFF_MI = OFF_MO + D_BRANCH
OFF_MF = OFF_MI + H_M
OFF_G = OFF_MF + H_M

F32 = jnp.float32
BF16 = jnp.bfloat16
NEG = -1e30
LANES = 128
V7X_VMEM_BUDGET = 56 * 1024 * 1024
HI = lax.Precision.HIGHEST


def _cparams(sem, vmem_mb):
    return pltpu.CompilerParams(dimension_semantics=sem,
                                vmem_limit_bytes=min(vmem_mb * 1024 * 1024, V7X_VMEM_BUDGET))


def _tile(n, pref):
    t = min(pref, n)
    while n % t:
        t //= 2
    return t


def _rnd(x):
    return x.astype(BF16).astype(F32)


def _rnd_bits(x):
    u = lax.bitcast_convert_type(x, jnp.uint32)
    r = (u + jnp.uint32(0x7FFF) + ((u >> 16) & jnp.uint32(1))) & jnp.uint32(0xFFFF0000)
    return lax.bitcast_convert_type(r, F32)


def _sigmoid(x):
    return 1.0 / (1.0 + jnp.exp(-x))


def _log_sigmoid(x):
    return jnp.minimum(x, 0.0) - jnp.log(1.0 + jnp.exp(-jnp.abs(x)))


def _layer_norm_rows(y, g, b):
    mu = jnp.mean(y, axis=-1, keepdims=True)
    yc = y - mu
    var = jnp.mean(yc * yc, axis=-1, keepdims=True)
    return yc * lax.rsqrt(var + LN_EPS) * g + b


def _proj_kernel(x_ref, w_ref, b_ref, o_ref, wbf_ref):
    @pl.when(pl.program_id(1) == 0)
    def _cast():
        wbf_ref[...] = w_ref[0].astype(BF16)

    acc = lax.dot_general(x_ref[...], wbf_ref[...], (((1,), (1,)), ((), ())), preferred_element_type=F32)
    o_ref[...] = (acc + b_ref[...]).astype(o_ref.dtype)


def _proj(x_bf, w_t, layer, row0, ncols, bias, out_dtype, tm=512, tn=1024):
    m, k = x_bf.shape
    tm = _tile(m, tm)
    tn = _tile(ncols, tn)
    return pl.pallas_call(
        _proj_kernel,
        grid=(ncols // tn, m // tm),
        in_specs=[pl.BlockSpec((tm, k), lambda j, i: (i, 0)),
                  pl.BlockSpec((pl.Element(1), pl.Element(tn), pl.Element(k)),
                               lambda j, i: (layer, pl.multiple_of(row0 + j * tn, 8), 0)),
                  pl.BlockSpec((1, tn), lambda j, i: (0, j))],
        out_specs=pl.BlockSpec((tm, tn), lambda j, i: (i, j)),
        out_shape=jax.ShapeDtypeStruct((m, ncols), out_dtype),
        scratch_shapes=[pltpu.VMEM((tn, k), BF16)],
        compiler_params=_cparams(("arbitrary", "arbitrary"), 48),
        name="proj",
    )(x_bf, w_t, bias)


def _conv_kernel(zb_ref, zc_ref, zx_ref, w_ref, buf_ref, y_ref, nb_ref, carry_ref):
    s_idx = pl.program_id(2)

    @pl.when(s_idx == 0)
    def _init():
        carry_ref[...] = buf_ref[...]

    u = zc_ref[...] * zx_ref[...]
    ts = u.shape[0]
    row = lax.broadcasted_iota(jnp.int32, u.shape, 0)
    c0 = carry_ref[0:1, :]
    c1 = carry_ref[1:2, :]
    u1 = jnp.where(row == 0, c1, pltpu.roll(u, 1, 0))
    u2 = jnp.where(row == 0, c0, jnp.where(row == 1, c1, pltpu.roll(u, 2, 0)))
    w = w_ref[...]
    y = u2 * w[0:1, :]
    y = y + u1 * w[1:2, :]
    y = y + u * w[2:3, :]
    y_ref[...] = (zb_ref[...] * y).astype(y_ref.dtype)
    last = zc_ref[ts - 2:ts, :] * zx_ref[ts - 2:ts, :]
    carry_ref[...] = last
    nb_ref[...] = last


def _conv_prompt(z_conv, conv_w, buf, bsz, seq):
    c = D_BRANCH
    tc = _tile(c, 512)
    ts = _tile(seq, 1024)
    ncb = c // tc
    nsb = seq // ts
    z3 = z_conv.reshape(bsz, seq, 3 * c)
    y, nb = pl.pallas_call(
        _conv_kernel,
        grid=(bsz, ncb, nsb),
        in_specs=[pl.BlockSpec((None, ts, tc), lambda b, j, s: (b, s, j)),
                  pl.BlockSpec((None, ts, tc), lambda b, j, s: (b, s, ncb + j)),
                  pl.BlockSpec((None, ts, tc), lambda b, j, s: (b, s, 2 * ncb + j)),
                  pl.BlockSpec((CONV_W, tc), lambda b, j, s: (0, j)),
                  pl.BlockSpec((None, CONV_W - 1, tc), lambda b, j, s: (b, 0, j))],
        out_specs=[pl.BlockSpec((None, ts, tc), lambda b, j, s: (b, s, j)),
                   pl.BlockSpec((None, CONV_W - 1, tc), lambda b, j, s: (b, 0, j))],
        out_shape=[jax.ShapeDtypeStruct((bsz, seq, c), BF16),
                   jax.ShapeDtypeStruct((bsz, CONV_W - 1, c), F32)],
        scratch_shapes=[pltpu.VMEM((CONV_W - 1, tc), F32)],
        compiler_params=_cparams(("arbitrary", "arbitrary", "arbitrary"), 40),
        name="conv",
    )(z3, z3, z3, conv_w, buf)
    return y.reshape(bsz * seq, c), nb


def _t5_bucket(dist):
    dist = jnp.maximum(dist, 0)
    max_exact = N_BUCKETS // 2
    large = max_exact + (jnp.log(jnp.maximum(dist, 1).astype(F32) / max_exact)
                         / math.log(MAX_DISTANCE / max_exact) * (N_BUCKETS - max_exact)).astype(jnp.int32)
    large = jnp.minimum(large, N_BUCKETS - 1)
    return jnp.where(dist < max_exact, dist, large)


def _bias_kernel(rb_ref, bk_ref, o_ref):
    h = pl.program_id(0)
    bk = bk_ref[...]
    acc = jnp.full(bk.shape, NEG, F32)
    for b in range(N_BUCKETS):
        acc = jnp.where(bk == b, rb_ref[b, h], acc)
    o_ref[...] = acc


def _bias_tiles(rel_bias, buckets):
    nk, r, c = buckets.shape
    return pl.pallas_call(
        _bias_kernel,
        grid=(H_DA, nk),
        in_specs=[pl.BlockSpec(memory_space=pltpu.SMEM),
                  pl.BlockSpec((None, r, c), lambda h, k: (k, 0, 0))],
        out_specs=pl.BlockSpec((None, None, r, c), lambda h, k: (h, k, 0, 0)),
        out_shape=jax.ShapeDtypeStruct((H_DA, nk, r, c), F32),
        compiler_params=_cparams(("arbitrary", "arbitrary"), 16),
        name="bias_tiles",
    )(rel_bias, buckets)


def _attn_tile_kernel(rb_ref, q_ref, k_ref, vt_ref, bias_ref, lam_ref, sub_ref, o_ref,
                      kbf, m1_ref, l1_ref, m2_ref, l2_ref, acc_ref, *, lam_init, t):
    h = pl.program_id(1)
    i = pl.program_id(2)

    @pl.when(i == 0)
    def _cast_keys():
        kbf[...] = k_ref[...].astype(BF16)

    q = q_ref[...] * (DK_DA ** -0.5)
    lane = lax.broadcasted_iota(jnp.int32, q.shape, 1)
    q1 = jnp.where(lane < DK_DA, q, 0.0).astype(BF16)
    q2 = jnp.where(lane >= DK_DA, q, 0.0).astype(BF16)
    lv = lam_ref[...]
    lam = (jnp.exp(jnp.sum(lv[0:1, :] * lv[1:2, :], axis=1, keepdims=True))
           - jnp.exp(jnp.sum(lv[2:3, :] * lv[3:4, :], axis=1, keepdims=True)) + lam_init)
    bfar = rb_ref[N_BUCKETS - 1, h]
    nt = (((1,), (1,)), ((), ()))
    streams = ((q1, m1_ref, l1_ref), (q2, m2_ref, l2_ref))

    def keys(j):
        return kbf[pl.ds(pl.multiple_of(j * t, t), t), :]

    def score(kb, qb):
        return lax.dot_general(kb, qb, nt, preferred_element_type=F32)

    def stats(j, tile):
        kb = keys(j)
        for qb, m_ref, l_ref in streams:
            s = score(kb, qb)
            m_old = m_ref[...]
            if tile is None:
                m_new = jnp.maximum(m_old, jnp.max(s, axis=0, keepdims=True) + bfar)
                e = jnp.exp(s - (m_new - bfar))
            else:
                s = s + tile
                m_new = jnp.maximum(m_old, jnp.max(s, axis=0, keepdims=True))
                e = jnp.exp(s - m_new)
            l_ref[...] = jnp.exp(m_old - m_new) * l_ref[...] + jnp.sum(e, axis=0, keepdims=True)
            m_ref[...] = m_new

    def apply(j, tile):
        kb = keys(j)
        ps = []
        for qb, m_ref, l_ref in streams:
            s = score(kb, qb)
            if tile is None:
                ps.append(jnp.exp(s - (m_ref[...] - bfar)) * l_ref[...])
            else:
                ps.append(jnp.exp(s + tile - m_ref[...]) * l_ref[...])
        a = ps[0] - lam * ps[1]
        acc_ref[...] += jnp.dot(vt_ref[j], a.astype(BF16), preferred_element_type=F32)

    def sweep(fn):
        lax.fori_loop(0, jnp.maximum(i - 1, 0), lambda j, c: (fn(j, None), c)[1], 0)

        @pl.when(i >= 1)
        def _sub_diagonal():
            fn(i - 1, bias_ref[0])

        fn(i, bias_ref[1])

    m1_ref[...] = jnp.full(m1_ref.shape, NEG, F32)
    m2_ref[...] = jnp.full(m2_ref.shape, NEG, F32)
    l1_ref[...] = jnp.zeros(l1_ref.shape, F32)
    l2_ref[...] = jnp.zeros(l2_ref.shape, F32)
    sweep(stats)
    l1_ref[...] = 1.0 / l1_ref[...]
    l2_ref[...] = 1.0 / l2_ref[...]
    acc_ref[...] = jnp.zeros(acc_ref.shape, F32)
    sweep(apply)
    ot = acc_ref[...]
    ms = jnp.mean(ot * ot, axis=0, keepdims=True)
    on = ot * lax.rsqrt(ms + LN_EPS) * sub_ref[...] * (1.0 - lam_init)
    o_ref[...] = on.T.astype(o_ref.dtype)


def _attn_tiles(q, k, vt, rel_bias, bias_near, lam_l, subln_col, bsz, seq, lam_init, t):
    nq = seq // t
    return pl.pallas_call(
        functools.partial(_attn_tile_kernel, lam_init=lam_init, t=t),
        grid=(bsz, H_DA, nq),
        in_specs=[pl.BlockSpec(memory_space=pltpu.SMEM),
                  pl.BlockSpec((None, t, DV_DA), lambda b, h, i: (b, i, h)),
                  pl.BlockSpec((None, seq, DV_DA), lambda b, h, i: (b, 0, h)),
                  pl.BlockSpec((None, None, nq, DV_DA, t), lambda b, h, i: (b, h, 0, 0, 0)),
                  pl.BlockSpec((None, 2, t, t), lambda b, h, i: (h, 0, 0, 0)),
                  pl.BlockSpec((4, DK_DA), lambda b, h, i: (0, 0)),
                  pl.BlockSpec((DV_DA, 1), lambda b, h, i: (0, 0))],
        out_specs=pl.BlockSpec((None, t, DV_DA), lambda b, h, i: (b, i, h)),
        out_shape=jax.ShapeDtypeStruct((bsz, seq, D_BRANCH), BF16),
        scratch_shapes=[pltpu.VMEM((seq, DV_DA), BF16),
                        pltpu.VMEM((1, t), F32), pltpu.VMEM((1, t), F32),
                        pltpu.VMEM((1, t), F32), pltpu.VMEM((1, t), F32),
                        pltpu.VMEM((DV_DA, t), F32)],
        compiler_params=_cparams(("arbitrary", "arbitrary", "arbitrary"), 48),
        name="attn",
    )(rel_bias, q, k, vt, bias_near, lam_l, subln_col)


def _mlstm_kernel(q_ref, k_ref, v_ref, o_ref, i_ref, f_ref, ct0_ref, n0_ref, m0_ref, nrm_ref,
                  h_ref, ct_ref, n_ref, m_ref, ct_s, n_s, m_s):
    c_idx = pl.program_id(2)

    @pl.when(c_idx == 0)
    def _init():
        ct_s[...] = ct0_ref[...]
        n_s[...] = n0_ref[...]
        m_s[...] = m0_ref[...]

    L = q_ref.shape[0]
    q = q_ref[...]
    k = k_ref[...] * (DQK_M ** -0.5)
    v = v_ref[...]
    i_row = i_ref[...]
    logf = _log_sigmoid(f_ref[...])
    r_io = lax.broadcasted_iota(jnp.int32, (L, L), 0)
    c_io = lax.broadcasted_iota(jnp.int32, (L, L), 1)
    upper = (r_io <= c_io).astype(F32)
    b_row = jnp.dot(jnp.broadcast_to(logf, (8, L)), upper, precision=HI,
                    preferred_element_type=F32)[0:1, :]
    stack = jnp.where(r_io == 0, b_row, jnp.where(r_io == 1, i_row, 0.0))
    stack_t = stack.T
    b_col = stack_t[:, 0:1]
    i_col = stack_t[:, 1:2]
    d = jnp.where(r_io >= c_io, b_col - b_row + i_row, NEG)
    m_prev = m_s[:, 0:1]
    inter = b_col + m_prev
    m_t = jnp.maximum(inter, jnp.max(d, axis=1, keepdims=True))
    w_inter = jnp.exp(inter - m_t)
    qb = q.astype(BF16)
    kb = k.astype(BF16)
    nt = (((1,), (1,)), ((), ()))
    qk = lax.dot_general(qb, kb, nt, preferred_element_type=F32) * jnp.exp(d - m_t)
    ct = ct_s[...]
    num = (jnp.dot(qk.astype(BF16), v.astype(BF16), preferred_element_type=F32)
           + w_inter * jnp.dot(qb, ct.astype(BF16), preferred_element_type=F32))
    n_row = n_s[...]
    den = jnp.sum(qk, axis=1, keepdims=True) + w_inter * jnp.sum(_rnd(q) * _rnd_bits(n_row), axis=1, keepdims=True)
    h = num / jnp.maximum(jnp.abs(den), jnp.exp(-m_t))
    m_new = m_t[L - 1:L, :]
    b_last = b_row[:, L - 1:L]
    w_end = jnp.exp(b_last - b_col + i_col - m_new)
    decay = jnp.exp(b_last + m_prev - m_new)
    ct_new = decay * ct + jnp.dot((w_end * k).T.astype(BF16), v.astype(BF16), preferred_element_type=F32)
    n_new = decay * n_row + jnp.sum(_rnd_bits(w_end) * _rnd(k), axis=0, keepdims=True)
    ct_s[...] = ct_new
    n_s[...] = n_new
    m_s[...] = jnp.broadcast_to(m_new, m_s.shape)
    ct_ref[...] = ct_new
    n_ref[...] = n_new
    m_ref[...] = jnp.broadcast_to(m_new, m_ref.shape)
    hn = h * lax.rsqrt(jnp.mean(h * h, axis=1, keepdims=True) + LN_EPS) * nrm_ref[...]
    h_ref[...] = (hn * _sigmoid(o_ref[...])).astype(h_ref.dtype)


def _mlstm_prompt(mqk, mv, mo, i_rows, f_rows, ct0, n0, m0, mnorm, bsz, seq):
    L = M_CHUNK if seq % M_CHUNK == 0 else seq
    nc = seq // L
    st = lambda b, h, c: (b, h, 0, 0)
    return pl.pallas_call(
        _mlstm_kernel,
        grid=(bsz, H_M, nc),
        in_specs=[pl.BlockSpec((None, L, DQK_M), lambda b, h, c: (b, c, h)),
                  pl.BlockSpec((None, L, DQK_M), lambda b, h, c: (b, c, H_M + h)),
                  pl.BlockSpec((None, L, DV_M), lambda b, h, c: (b, c, h)),
                  pl.BlockSpec((None, L, DV_M), lambda b, h, c: (b, c, h)),
                  pl.BlockSpec((None, None, 1, L), lambda b, h, c: (b, h, 0, c)),
                  pl.BlockSpec((None, None, 1, L), lambda b, h, c: (b, h, 0, c)),
                  pl.BlockSpec((None, None, DQK_M, DV_M), st),
                  pl.BlockSpec((None, None, 1, DQK_M), st),
                  pl.BlockSpec((None, None, 1, LANES), st),
                  pl.BlockSpec((1, DV_M), lambda b, h, c: (0, 0))],
        out_specs=[pl.BlockSpec((None, L, DV_M), lambda b, h, c: (b, c, h)),
                   pl.BlockSpec((None, None, DQK_M, DV_M), st),
                   pl.BlockSpec((None, None, 1, DQK_M), st),
                   pl.BlockSpec((None, None, 1, LANES), st)],
        out_shape=[jax.ShapeDtypeStruct((bsz, seq, D_BRANCH), BF16),
                   jax.ShapeDtypeStruct((bsz, H_M, DQK_M, DV_M), F32),
                   jax.ShapeDtypeStruct((bsz, H_M, 1, DQK_M), F32),
                   jax.ShapeDtypeStruct((bsz, H_M, 1, LANES), F32)],
        scratch_shapes=[pltpu.VMEM((DQK_M, DV_M), F32), pltpu.VMEM((1, DQK_M), F32), pltpu.VMEM((1, LANES), F32)],
        compiler_params=_cparams(("arbitrary", "arbitrary", "arbitrary"), 32),
        name="mlstm",
    )(mqk, mqk, mv, mo, i_rows, f_rows, ct0, n0, m0, mnorm)


def _merge_kernel(b0_ref, b1_ref, b2_ref, g0_ref, g1_ref, g2_ref, w_ref, o_ref, wbf_ref):
    @pl.when(pl.program_id(1) == 0)
    def _cast():
        wbf_ref[...] = w_ref[...].astype(BF16)

    acc = _sigmoid(g0_ref[...]) * jnp.dot(b0_ref[...], wbf_ref[0], preferred_element_type=F32)
    acc = acc + _sigmoid(g1_ref[...]) * jnp.dot(b1_ref[...], wbf_ref[1], preferred_element_type=F32)
    acc = acc + _sigmoid(g2_ref[...]) * jnp.dot(b2_ref[...], wbf_ref[2], preferred_element_type=F32)
    o_ref[...] = acc.astype(o_ref.dtype)


def _merge_prompt(br0, br1, br2, zg, w_branch, layer, tm=512, tn=512):
    m, kb = br0.shape
    d = zg.shape[1] // N_BRANCH
    tm = _tile(m, tm)
    tn = _tile(d, tn)
    nn = d // tn
    bspec = pl.BlockSpec((tm, kb), lambda j, i: (i, 0))
    gspec = lambda n: pl.BlockSpec((tm, tn), lambda j, i: (i, n * nn + j))
    return pl.pallas_call(
        _merge_kernel,
        grid=(nn, m // tm),
        in_specs=[bspec, bspec, bspec, gspec(0), gspec(1), gspec(2),
                  pl.BlockSpec((None, N_BRANCH, kb, tn), lambda j, i: (layer, 0, 0, j))],
        out_specs=pl.BlockSpec((tm, tn), lambda j, i: (i, j)),
        out_shape=jax.ShapeDtypeStruct((m, d), BF16),
        scratch_shapes=[pltpu.VMEM((N_BRANCH, kb, tn), BF16)],
        compiler_params=_cparams(("arbitrary", "arbitrary"), 48),
        name="merge",
    )(br0, br1, br2, zg, zg, zg, w_branch)


def _outproj_ln_kernel(a_ref, w_ref, x_ref, g_ref, b_ref, o_ref, obf_ref, acc_ref, *, alpha, tn):
    j = pl.program_id(1)
    nn = acc_ref.shape[0]
    acc_ref[j] = jnp.dot(a_ref[...], w_ref[...].astype(BF16), preferred_element_type=F32)

    @pl.when(j == nn - 1)
    def _ln():
        d = nn * tn
        ys = [alpha * x_ref[:, c * tn:(c + 1) * tn] + acc_ref[c] for c in range(nn)]
        mu = sum(jnp.sum(y, axis=-1, keepdims=True) for y in ys) / d
        ys = [y - mu for y in ys]
        var = sum(jnp.sum(y * y, axis=-1, keepdims=True) for y in ys) / d
        inv = lax.rsqrt(var + LN_EPS)
        for c in range(nn):
            sl = slice(c * tn, (c + 1) * tn)
            out = ys[c] * inv * g_ref[:, sl] + b_ref[:, sl]
            o_ref[:, sl] = out
            obf_ref[:, sl] = out.astype(BF16)


def _outproj_ln(a_bf, w, layer, x, g, b, alpha, tm=512, tn=512):
    m, k = a_bf.shape
    d = x.shape[1]
    tm = _tile(m, tm)
    tn = _tile(d, tn)
    row = lambda i, j: (i, 0)
    return pl.pallas_call(
        functools.partial(_outproj_ln_kernel, alpha=alpha, tn=tn),
        grid=(m // tm, d // tn),
        in_specs=[pl.BlockSpec((tm, k), row),
                  pl.BlockSpec((None, k, tn), lambda i, j: (layer, 0, j)),
                  pl.BlockSpec((tm, d), row),
                  pl.BlockSpec((1, d), lambda i, j: (0, 0)),
                  pl.BlockSpec((1, d), lambda i, j: (0, 0))],
        out_specs=[pl.BlockSpec((tm, d), row), pl.BlockSpec((tm, d), row)],
        out_shape=[jax.ShapeDtypeStruct((m, d), F32), jax.ShapeDtypeStruct((m, d), BF16)],
        scratch_shapes=[pltpu.VMEM((d // tn, tm, tn), F32)],
        compiler_params=_cparams(("arbitrary", "arbitrary"), 48),
        name="outproj_ln",
    )(a_bf, w, x, g, b)


def _route(z):
    lane = lax.broadcasted_iota(jnp.int32, z.shape, 1)
    lane_f = lane.astype(F32)
    big = float(LANES)
    gmask = lane < N_GROUPS
    lg = jnp.where(gmask, z, NEG)
    gmax = jnp.max(lg, axis=1, keepdims=True)
    gsel = jnp.min(jnp.where(gmask & (lg == gmax), lane_f, big), axis=1, keepdims=True)
    pg_sel = 1.0 / jnp.sum(jnp.where(gmask, jnp.exp(lg - gmax), 0.0), axis=1, keepdims=True)
    lo = N_GROUPS + EXP_PER_GROUP * gsel
    emask = (lane_f >= lo) & (lane_f < lo + EXP_PER_GROUP)
    le = jnp.where(emask, z, NEG)
    emax = jnp.max(le, axis=1, keepdims=True)
    pe = jnp.where(emask, jnp.exp(le - emax), 0.0)
    pe = pe / jnp.sum(pe, axis=1, keepdims=True)
    pm = jnp.where(emask, pe, -1.0)
    p1 = jnp.max(pm, axis=1, keepdims=True)
    i1 = jnp.min(jnp.where(pm == p1, lane_f, big), axis=1, keepdims=True)
    pm2 = jnp.where(lane_f == i1, -1.0, pm)
    p2 = jnp.max(pm2, axis=1, keepdims=True)
    i2 = jnp.min(jnp.where(pm2 == p2, lane_f, big), axis=1, keepdims=True)
    tot = p1 + p2
    w1 = pg_sel * p1 / tot
    w2 = pg_sel * p2 / tot
    wts = jnp.where(lane == 0, w1, jnp.where(lane == 1, w2, 0.0))
    ids = jnp.where(lane == 0, i1 - N_GROUPS, jnp.where(lane == 1, i2 - N_GROUPS, 0.0)).astype(jnp.int32)
    return wts, ids


def _router_kernel(x_ref, w_ref, b_ref, wt_ref, id_ref):
    z = jnp.dot(x_ref[...], w_ref[...].astype(BF16), preferred_element_type=F32) + b_ref[...]
    wts, ids = _route(z)
    wt_ref[...] = wts
    id_ref[...] = ids


def _router(x, w_r, b_r, tm=256):
    m, k = x.shape
    tm = _tile(m, tm)
    return pl.pallas_call(
        _router_kernel,
        grid=(m // tm,),
        in_specs=[pl.BlockSpec((tm, k), lambda i: (i, 0)),
                  pl.BlockSpec((k, LANES), lambda i: (0, 0)),
                  pl.BlockSpec((1, LANES), lambda i: (0, 0))],
        out_specs=[pl.BlockSpec((tm, LANES), lambda i: (i, 0)), pl.BlockSpec((tm, LANES), lambda i: (i, 0))],
        out_shape=[jax.ShapeDtypeStruct((m, LANES), F32), jax.ShapeDtypeStruct((m, LANES), jnp.int32)],
        compiler_params=_cparams(("arbitrary",), 32),
        name="router",
    )(x, w_r, b_r)


def _router_weights(w_rg, b_rg, w_re, b_re):
    k = w_rg.shape[0]
    w = jnp.zeros((k, LANES), F32).at[:, :N_GROUPS].set(w_rg).at[:, N_GROUPS:N_GROUPS + N_EXPERTS].set(w_re)
    b = jnp.zeros((1, LANES), F32).at[0, :N_GROUPS].set(b_rg).at[0, N_GROUPS:N_GROUPS + N_EXPERTS].set(b_re)
    return w, b


def _row_copy(src_hbm, row, dst_ref, r, sem):
    return pltpu.make_async_copy(src_hbm.at[pl.ds(row, 1)], dst_ref.at[pl.ds(r, 1)], sem)


def _gather_kernel(src_ref, x_hbm, o_ref, sem):
    i = pl.program_id(0)
    te = o_ref.shape[0]

    def issue(r, c):
        _row_copy(x_hbm, src_ref[i * te + r], o_ref, r, sem).start()
        return c

    lax.fori_loop(0, te, issue, 0)

    def wait(r, c):
        _row_copy(x_hbm, 0, o_ref, r, sem).wait()
        return c

    lax.fori_loop(0, te, wait, 0)


def _gather_rows(x, row_src, te):
    nr = row_src.shape[0]
    d = x.shape[1]
    grid_spec = pltpu.PrefetchScalarGridSpec(
        num_scalar_prefetch=1,
        grid=(nr // te,),
        in_specs=[pl.BlockSpec(memory_space=pl.ANY)],
        out_specs=pl.BlockSpec((te, d), lambda i, s: (i, 0)),
        scratch_shapes=[pltpu.SemaphoreType.DMA(())])
    return pl.pallas_call(
        _gather_kernel,
        grid_spec=grid_spec,
        out_shape=jax.ShapeDtypeStruct((nr, d), x.dtype),
        compiler_params=_cparams(("arbitrary",), 32),
        name="moe_gather",
    )(row_src, x)


def _expert_kernel(te_ref, first_ref, nused_ref, xs_ref, rw_ref, wg_ref, wu_ref, wd_ref, y_ref,
                   wg_bf, wu_bf, wd_bf):
    i = pl.program_id(0)

    @pl.when(first_ref[i] == 1)
    def _cast():
        wg_bf[...] = wg_ref[...].astype(BF16)
        wu_bf[...] = wu_ref[...].astype(BF16)
        wd_bf[...] = wd_ref[...].astype(BF16)

    @pl.when(i < nused_ref[0])
    def _compute():
        xb = xs_ref[...].astype(BF16)
        g = jnp.dot(xb, wg_bf[...], preferred_element_type=F32)
        u = jnp.dot(xb, wu_bf[...], preferred_element_type=F32)
        hid = g * _sigmoid(g) * u * rw_ref[...]
        y_ref[...] = jnp.dot(hid.astype(BF16), wd_bf[...], preferred_element_type=F32)

    @pl.when(i >= nused_ref[0])
    def _pad():
        y_ref[...] = jnp.zeros(y_ref.shape, y_ref.dtype)


def _expert_mlp(xs, row_w, tile_expert, tile_first, n_used, w_gate, w_up, w_down, layer, te):
    nr, d = xs.shape
    f = w_gate.shape[-1]
    grid_spec = pltpu.PrefetchScalarGridSpec(
        num_scalar_prefetch=3,
        grid=(nr // te,),
        in_specs=[pl.BlockSpec((te, d), lambda i, e, fi, nu: (i, 0)),
                  pl.BlockSpec((te, 1), lambda i, e, fi, nu: (i, 0)),
                  pl.BlockSpec((None, None, d, f), lambda i, e, fi, nu: (layer, e[i], 0, 0)),
                  pl.BlockSpec((None, None, d, f), lambda i, e, fi, nu: (layer, e[i], 0, 0)),
                  pl.BlockSpec((None, None, f, d), lambda i, e, fi, nu: (layer, e[i], 0, 0))],
        out_specs=pl.BlockSpec((te, d), lambda i, e, fi, nu: (i, 0)),
        scratch_shapes=[pltpu.VMEM((d, f), BF16), pltpu.VMEM((d, f), BF16), pltpu.VMEM((f, d), BF16)])
    return pl.pallas_call(
        _expert_kernel,
        grid_spec=grid_spec,
        out_shape=jax.ShapeDtypeStruct((nr, d), F32),
        compiler_params=_cparams(("arbitrary",), 52),
        name="moe_experts",
    )(tile_expert, tile_first, n_used, xs, row_w, w_gate, w_up, w_down)


def _combine_ln_kernel(pos_ref, x_ref, ys_hbm, g_ref, b_ref, o_ref, obf_ref, buf0, buf1, sem, *, alpha):
    i = pl.program_id(0)
    tm = x_ref.shape[0]

    def issue(r, c):
        t = i * tm + r
        _row_copy(ys_hbm, pos_ref[2 * t], buf0, r, sem.at[0]).start()
        _row_copy(ys_hbm, pos_ref[2 * t + 1], buf1, r, sem.at[1]).start()
        return c

    lax.fori_loop(0, tm, issue, 0)

    def wait(r, c):
        _row_copy(ys_hbm, 0, buf0, r, sem.at[0]).wait()
        _row_copy(ys_hbm, 0, buf1, r, sem.at[1]).wait()
        return c

    lax.fori_loop(0, tm, wait, 0)
    y = alpha * x_ref[...] + (buf0[...] + buf1[...])
    out = _layer_norm_rows(y, g_ref[...], b_ref[...])
    o_ref[...] = out
    obf_ref[...] = out.astype(BF16)


def _combine_ln(x, ys, pos, g, b, alpha, tm=256):
    m, d = x.shape
    tm = _tile(m, tm)
    grid_spec = pltpu.PrefetchScalarGridSpec(
        num_scalar_prefetch=1,
        grid=(m // tm,),
        in_specs=[pl.BlockSpec((tm, d), lambda i, p: (i, 0)),
                  pl.BlockSpec(memory_space=pl.ANY),
                  pl.BlockSpec((1, d), lambda i, p: (0, 0)),
                  pl.BlockSpec((1, d), lambda i, p: (0, 0))],
        out_specs=[pl.BlockSpec((tm, d), lambda i, p: (i, 0)), pl.BlockSpec((tm, d), lambda i, p: (i, 0))],
        scratch_shapes=[pltpu.VMEM((tm, d), F32), pltpu.VMEM((tm, d), F32), pltpu.SemaphoreType.DMA((2,))])
    return pl.pallas_call(
        functools.partial(_combine_ln_kernel, alpha=alpha),
        grid_spec=grid_spec,
        out_shape=[jax.ShapeDtypeStruct((m, d), F32), jax.ShapeDtypeStruct((m, d), BF16)],
        compiler_params=_cparams(("arbitrary",), 40),
        name="moe_combine_ln",
    )(pos, x, ys, g, b)


def _dispatch_plan(ids, wts, te):
    t = ids.shape[0]
    a = 2 * t
    nr = a + N_EXPERTS * te
    e_flat = ids.reshape(a)
    w_flat = wts.reshape(a)
    order = jnp.argsort(e_flat, stable=True).astype(jnp.int32)
    onehot = jax.nn.one_hot(e_flat, N_EXPERTS, dtype=jnp.int32)
    counts = jnp.sum(onehot, axis=0)
    padded = ((counts + te - 1) // te) * te
    pad_end = jnp.cumsum(padded)
    pad_off = pad_end - padded
    off = jnp.cumsum(counts) - counts
    rank = jnp.sum((jnp.cumsum(onehot, axis=0) - onehot) * onehot, axis=1)
    pos = (pad_off[e_flat] + rank).astype(jnp.int32)
    n_used = (pad_end[-1] // te).astype(jnp.int32)
    tile_start = jnp.arange(nr // te, dtype=jnp.int32) * te
    tile_expert = jnp.sum((pad_end[None, :] <= tile_start[:, None]).astype(jnp.int32), axis=1)
    rows = jnp.arange(nr, dtype=jnp.int32)
    e_row = jnp.minimum(jnp.repeat(tile_expert, te), N_EXPERTS - 1)
    local = rows - pad_off[e_row]
    valid = (rows < pad_end[-1]) & (local < counts[e_row])
    src = order[jnp.clip(off[e_row] + local, 0, a - 1)]
    row_src = jnp.where(valid, src // 2, 0).astype(jnp.int32)
    row_w = jnp.where(valid, w_flat[src], 0.0).reshape(nr, 1)
    last_e = jnp.max(e_flat).astype(jnp.int32)
    tile_expert = jnp.minimum(tile_expert, last_e).astype(jnp.int32)
    prev = jnp.concatenate([jnp.full((1,), -1, jnp.int32), tile_expert[:-1]])
    tile_first = (tile_expert != prev).astype(jnp.int32)
    return row_src, row_w, pos, tile_expert, tile_first, n_used.reshape(1)


def _small_mm_kernel(x_ref, w_ref, b_ref, o_ref):
    o_ref[...] = jnp.dot(x_ref[...].astype(BF16), w_ref[...].astype(BF16), preferred_element_type=F32) + b_ref[...]


def _small_mm_t_kernel(x_ref, w_ref, b_ref, o_ref):
    o_ref[...] = lax.dot_general(x_ref[...].astype(BF16), w_ref[...].astype(BF16), (((1,), (1,)), ((), ())),
                                 preferred_element_type=F32) + b_ref[...]


def _small_mm(x, w, lead, bias=None, tn=1024, transposed=False):
    r, k = x.shape
    n = w.shape[-2] if transposed else w.shape[-1]
    tn = min(tn, n)
    nlead = len(lead)
    if bias is None:
        bias = jnp.zeros((1, n), F32)
    if transposed:
        w_spec = pl.BlockSpec((None,) * nlead + (tn, k), lambda j: tuple(lead) + (j, 0))
    else:
        w_spec = pl.BlockSpec((None,) * nlead + (k, tn), lambda j: tuple(lead) + (0, j))
    return pl.pallas_call(
        _small_mm_t_kernel if transposed else _small_mm_kernel,
        grid=(pl.cdiv(n, tn),),
        in_specs=[pl.BlockSpec((r, k), lambda j: (0, 0)),
                  w_spec,
                  pl.BlockSpec((1, tn), lambda j: (0, j))],
        out_specs=pl.BlockSpec((r, tn), lambda j: (0, j)),
        out_shape=jax.ShapeDtypeStruct((r, n), F32),
        compiler_params=_cparams(("arbitrary",), 40),
        name="small_mm",
    )(x, w, bias)


def _small_call(fn, out_shapes, *args, name="small"):
    n_in = len(args)

    def kern(*refs):
        outs = fn(*[r[...] for r in refs[:n_in]])
        for r, o in zip(refs[n_in:], outs):
            r[...] = o

    return pl.pallas_call(kern, out_shape=out_shapes, name=name)(*args)


def _sample_conv_fn(zb, zc, zx, buf0, buf1, w):
    u = zc * zx
    y = buf0 * w[0:1, :]
    y = y + buf1 * w[1:2, :]
    y = y + u * w[2:3, :]
    return zb * y, u


def _sample_mlstm_kernel(q_ref, k_ref, v_ref, vc_ref, o_ref, i_ref, f_ref, c_ref, n_ref, m_ref, nrm_ref,
                         h_ref, c_out, n_out, m_out):
    for hh in range(H_M):
        q = q_ref[hh]
        k = k_ref[hh] * (DQK_M ** -0.5)
        v = v_ref[hh]
        v_col = vc_ref[hh]
        ig = i_ref[hh][:, 0:1]
        logf = _log_sigmoid(f_ref[hh][:, 0:1])
        m_prev = m_ref[hh][:, 0:1]
        c = c_ref[hh]
        n = n_ref[hh]
        inter = logf + m_prev
        m_t = jnp.maximum(inter, ig)
        w_inter = jnp.exp(inter - m_t)
        qk = jnp.sum(q * k, axis=1, keepdims=True) * jnp.exp(ig - m_t)
        cq = lax.dot_general(jnp.broadcast_to(q, (8, DQK_M)).astype(BF16), c.astype(BF16),
                             (((1,), (1,)), ((), ())), preferred_element_type=F32)[0:1, :]
        num = qk * v + w_inter * cq
        den = qk + w_inter * jnp.sum(n * q, axis=1, keepdims=True)
        h = num / jnp.maximum(jnp.abs(den), jnp.exp(-m_t))
        w_end = jnp.exp(ig - m_t)
        decay = jnp.exp(logf + m_prev - m_t)
        c_out[hh] = decay * c + w_end * (v_col * k)
        n_out[hh] = decay * n + w_end * k
        m_out[hh] = jnp.broadcast_to(m_t, (1, LANES))
        hn = h * lax.rsqrt(jnp.mean(h * h, axis=1, keepdims=True) + LN_EPS) * nrm_ref[...]
        h_ref[hh] = hn * _sigmoid(o_ref[hh])


def _sample_mlstm(mq, mk, mv, mo, mi, mf, c0, n0, m0, mnorm):
    r = mq.shape[0]
    mv_col = mv.reshape(r, H_M, DV_M, 1)
    blk = lambda *s: pl.BlockSpec((None,) + s, lambda b: (b,) + (0,) * len(s))
    return pl.pallas_call(
        _sample_mlstm_kernel,
        grid=(r,),
        in_specs=[blk(H_M, 1, DQK_M), blk(H_M, 1, DQK_M), blk(H_M, 1, DV_M), blk(H_M, DV_M, 1), blk(H_M, 1, DV_M),
                  blk(H_M, 1, LANES), blk(H_M, 1, LANES), blk(H_M, DV_M, DQK_M), blk(H_M, 1, DQK_M),
                  blk(H_M, 1, LANES), pl.BlockSpec((1, DV_M), lambda b: (0, 0))],
        out_specs=[blk(H_M, 1, DV_M), blk(H_M, DV_M, DQK_M), blk(H_M, 1, DQK_M), blk(H_M, 1, LANES)],
        out_shape=[jax.ShapeDtypeStruct((r, H_M, 1, DV_M), F32),
                   jax.ShapeDtypeStruct((r, H_M, DV_M, DQK_M), F32),
                   jax.ShapeDtypeStruct((r, H_M, 1, DQK_M), F32),
                   jax.ShapeDtypeStruct((r, H_M, 1, LANES), F32)],
        compiler_params=_cparams(("arbitrary",), 32),
        name="sample_mlstm",
    )(mq, mk, mv, mv_col, mo, mi, mf, c0, n0, m0, mnorm)


def _class_reduce(x, op):
    sh = H_DA
    while sh < x.shape[-1]:
        x = op(x, pltpu.roll(x, sh, x.ndim - 1))
        sh *= 2
    return x


def _paged_attn_kernel(pt_ref, q_ref, kn_ref, vn_ref, *rest, lam_init, group):
    k_refs, v_refs = rest[:group], rest[group:2 * group]
    (brow_ref, bself_ref, lam_ref, sub_ref, o_ref, qsel, sc, m_run, l_run, mb, ilb, acc, lam_s) = rest[2 * group:]
    sweep = pl.program_id(1)
    p = pl.program_id(2)
    last_step = pl.num_programs(2) - 1
    rows = k_refs[0].shape[0]
    nt = (((1,), (1,)), ((), ()))
    r_io = lax.broadcasted_iota(jnp.int32, (2 * H_DA, rows), 0)
    c_io = lax.broadcasted_iota(jnp.int32, (2 * H_DA, rows), 1)
    own = (c_io % H_DA) == (r_io % H_DA)
    lane = lax.broadcasted_iota(jnp.int32, (H_DA, DV_DA), 1)
    first = lane < DK_DA

    def to_lanes(col):
        return jnp.sum(jnp.where(own[0:H_DA], col, 0.0), axis=0, keepdims=True)

    def to_col(row):
        return jnp.max(jnp.where(own[0:H_DA], row, NEG), axis=1, keepdims=True)

    def self_scores():
        prod = _rnd_bits(kn_ref[...]) * _rnd_bits(q_ref[...] * (DK_DA ** -0.5))
        t1 = jnp.sum(jnp.where(first, prod, 0.0), axis=1, keepdims=True) + bself_ref[...]
        t2 = jnp.sum(jnp.where(first, 0.0, prod), axis=1, keepdims=True) + bself_ref[...]
        return t1, t2

    @pl.when((sweep == 0) & (p == 0))
    def _init():
        q = q_ref[...] * (DK_DA ** -0.5)
        qsel[...] = jnp.concatenate([jnp.where(first, q, 0.0), jnp.where(first, 0.0, q)], axis=0).astype(BF16)
        m_run[...] = jnp.full(m_run.shape, NEG, F32)
        l_run[...] = jnp.zeros(l_run.shape, F32)
        lv = lam_ref[...]
        lam = (jnp.exp(jnp.sum(lv[0:1, :] * lv[1:2, :], axis=1, keepdims=True))
               - jnp.exp(jnp.sum(lv[2:3, :] * lv[3:4, :], axis=1, keepdims=True)) + lam_init)
        lam_s[...] = jnp.broadcast_to(lam, lam_s.shape)

    @pl.when(sweep == 0)
    def _scores():
        for i, k_ref in enumerate(k_refs):
            r = lax.dot_general(qsel[...], k_ref[...].astype(BF16), nt, preferred_element_type=F32)
            r = jnp.where(own, r, 0.0)
            if i == group - 1:
                bias = jnp.where(p == last_step, brow_ref[1:2, :], brow_ref[0:1, :])
            else:
                bias = brow_ref[0:1, :]
            for c in range(2):
                s = jnp.sum(r[c * H_DA:(c + 1) * H_DA], axis=0, keepdims=True) + bias
                sc[p * group + i, c:c + 1, :] = s
                m_old = m_run[c:c + 1, :]
                m_new = jnp.maximum(m_old, s)
                l_run[c:c + 1, :] = l_run[c:c + 1, :] * jnp.exp(m_old - m_new) + jnp.exp(s - m_new)
                m_run[c:c + 1, :] = m_new

        @pl.when(p == last_step)
        def _close():
            for c, t in enumerate(self_scores()):
                tb = to_lanes(t)
                m_c = m_run[c:c + 1, :]
                m_all = jnp.maximum(_class_reduce(m_c, jnp.maximum), tb)
                l_all = _class_reduce(l_run[c:c + 1, :] * jnp.exp(m_c - m_all), jnp.add) + jnp.exp(tb - m_all)
                mb[c:c + 1, :] = m_all
                ilb[c:c + 1, :] = 1.0 / l_all
            acc[...] = jnp.zeros(acc.shape, F32)

    @pl.when(sweep == 1)
    def _apply():
        lam = lam_s[:, 0:1]
        part = jnp.zeros(acc.shape, F32)
        for i, v_ref in enumerate(v_refs):
            pg = p * group + i
            w1 = jnp.exp(sc[pg, 0:1, :] - mb[0:1, :]) * ilb[0:1, :]
            w2 = jnp.exp(sc[pg, 1:2, :] - mb[1:2, :]) * ilb[1:2, :]
            a = jnp.where(own, w1 - lam * w2, 0.0).astype(BF16)
            part = part + jnp.dot(a, v_ref[...].astype(BF16), preferred_element_type=F32)
        acc[...] += part

        @pl.when(p == last_step)
        def _fin():
            t1, t2 = self_scores()
            w1s = jnp.exp(t1 - to_col(mb[0:1, :])) * to_col(ilb[0:1, :])
            w2s = jnp.exp(t2 - to_col(mb[1:2, :])) * to_col(ilb[1:2, :])
            o = acc[0:H_DA, :] + _rnd_bits(w1s - lam * w2s) * _rnd_bits(vn_ref[...])
            on = o * lax.rsqrt(jnp.mean(o * o, axis=1, keepdims=True) + LN_EPS) * sub_ref[...] * (1.0 - lam_init)
            o_ref[...] = on


def _paged_attn(q, k_new, v_new, cache_k, cache_v, page_table, layer, bias_rows, bias_self, lam_l, subln_row,
                lam_init):
    r = q.shape[0]
    npg = page_table.shape[1]
    psz = cache_k.shape[2]
    rows = psz * H_DA
    pt = page_table.reshape(-1)
    tok = lambda b, s, p, t: (b, 0, 0)
    cst2 = lambda b, s, p, t: (0, 0)
    group = _tile(npg, 8)
    nsteps = npg // group

    def kpage(i):
        return lambda b, s, p, t: (layer, t[b * npg + (p * (1 - s) + (nsteps - 1) * s) * group + i], 0, 0)

    def vpage(i):
        return lambda b, s, p, t: (layer, t[b * npg + p * s * group + i], 0, 0)

    pool = cache_k.shape[1]
    cache_k = cache_k.reshape(cache_k.shape[0], pool, rows, DV_DA)
    cache_v = cache_v.reshape(cache_v.shape[0], pool, rows, DV_DA)
    grid_spec = pltpu.PrefetchScalarGridSpec(
        num_scalar_prefetch=1,
        grid=(r, 2, nsteps),
        in_specs=[pl.BlockSpec((None, H_DA, DV_DA), tok),
                  pl.BlockSpec((None, H_DA, DV_DA), tok),
                  pl.BlockSpec((None, H_DA, DV_DA), tok)]
                 + [pl.BlockSpec((None, None, rows, DV_DA), kpage(i)) for i in range(group)]
                 + [pl.BlockSpec((None, None, rows, DV_DA), vpage(i)) for i in range(group)]
                 + [pl.BlockSpec((2, rows), cst2),
                  pl.BlockSpec((H_DA, 1), cst2),
                  pl.BlockSpec((4, DK_DA), cst2),
                  pl.BlockSpec((1, DV_DA), cst2)],
        out_specs=pl.BlockSpec((None, H_DA, DV_DA), tok),
        scratch_shapes=[pltpu.VMEM((2 * H_DA, DV_DA), BF16), pltpu.VMEM((npg, 2, rows), F32),
                        pltpu.VMEM((2, rows), F32), pltpu.VMEM((2, rows), F32),
                        pltpu.VMEM((2, rows), F32), pltpu.VMEM((2, rows), F32),
                        pltpu.VMEM((2 * H_DA, DV_DA), F32), pltpu.VMEM((1, LANES), F32)])
    return pl.pallas_call(
        functools.partial(_paged_attn_kernel, lam_init=lam_init, group=group),
        grid_spec=grid_spec,
        out_shape=jax.ShapeDtypeStruct((r, H_DA, DV_DA), F32),
        compiler_params=_cparams(("arbitrary", "arbitrary", "arbitrary"), 40),
        name="paged_attn",
    )(pt, q, k_new, v_new, *([cache_k] * group), *([cache_v] * group), bias_rows, bias_self, lam_l, subln_row)


def _sample_moe_kernel(x_ref, cmb_ref, wg_ref, wu_ref, wd_ref, y_ref):
    e = pl.program_id(0)

    @pl.when(e == 0)
    def _init():
        y_ref[...] = jnp.zeros(y_ref.shape, F32)

    x = x_ref[...].astype(BF16)
    g = jnp.dot(x, wg_ref[...].astype(BF16), preferred_element_type=F32)
    u = jnp.dot(x, wu_ref[...].astype(BF16), preferred_element_type=F32)
    hid = g * _sigmoid(g) * u * cmb_ref[...]
    y_ref[...] += jnp.dot(hid.astype(BF16), wd_ref[...].astype(BF16), preferred_element_type=F32)


def _sample_moe(x, combine_t, w_gate, w_up, w_down, layer):
    r, d = x.shape
    f = w_gate.shape[-1]
    return pl.pallas_call(
        _sample_moe_kernel,
        grid=(N_EXPERTS,),
        in_specs=[pl.BlockSpec((r, d), lambda e: (0, 0)),
                  pl.BlockSpec((None, r, 1), lambda e: (e, 0, 0)),
                  pl.BlockSpec((None, None, d, f), lambda e: (layer, e, 0, 0)),
                  pl.BlockSpec((None, None, d, f), lambda e: (layer, e, 0, 0)),
                  pl.BlockSpec((None, None, f, d), lambda e: (layer, e, 0, 0))],
        out_specs=pl.BlockSpec((r, d), lambda e: (0, 0)),
        out_shape=jax.ShapeDtypeStruct((r, d), F32),
        compiler_params=_cparams(("arbitrary",), 48),
        name="sample_moe",
    )(x, combine_t, w_gate, w_up, w_down)


def _prompt_layer(l, x, x_bf, bsz, seq, bias_t, attn_t, moe_te, p):
    f32 = F32
    alpha = (2.0 * p["depth"]) ** 0.25
    lam_init = 0.8 - 0.6 * math.exp(-0.3 * l)
    t = bsz * seq
    w_in_t, b_in = p["w_in_t"], p["b_in"]
    bseg = lambda o, n: b_in[l, o:o + n].reshape(1, n)
    pr = lambda o, n, dt=f32: _proj(x_bf, w_in_t, l, o, n, bseg(o, n), dt)
    z_conv = pr(OFF_CONV, 3 * D_BRANCH)
    a_q = pr(OFF_AQ, D_BRANCH)
    a_k = pr(OFF_AK, D_BRANCH)
    a_v = pr(OFF_AV, D_BRANCH)
    m_qk = pr(OFF_MQ, 2 * H_M * DQK_M)
    m_v = pr(OFF_MV, D_BRANCH)
    m_o = pr(OFF_MO, D_BRANCH)
    z_g = pr(OFF_G, N_BRANCH * x.shape[1])
    z_if = pr(OFF_MI, LANES)

    y_conv, conv_buf = _conv_prompt(z_conv, p["conv_w"][l], jnp.zeros((bsz, CONV_W - 1, D_BRANCH), f32), bsz, seq)

    vt = a_v.reshape(bsz, seq // attn_t, attn_t, H_DA, DV_DA).transpose(0, 3, 1, 4, 2).astype(BF16)
    o = _attn_tiles(a_q.reshape(bsz, seq, D_BRANCH), a_k.reshape(bsz, seq, D_BRANCH), vt, p["rel_bias"], bias_t,
                    p["attn_lambda"][l], p["attn_subln"][l].reshape(DV_DA, 1), bsz, seq, lam_init, attn_t)
    o = o.reshape(t, D_BRANCH)

    gates = z_if[:, :2 * H_M].reshape(bsz, seq, 2, H_M).transpose(2, 0, 3, 1).reshape(2, bsz, H_M, 1, seq)
    h, ct1, n1, m1 = _mlstm_prompt(
        m_qk.reshape(bsz, seq, D_BRANCH), m_v.reshape(bsz, seq, D_BRANCH), m_o.reshape(bsz, seq, D_BRANCH),
        gates[0], gates[1],
        jnp.zeros((bsz, H_M, DQK_M, DV_M), f32), jnp.zeros((bsz, H_M, 1, DQK_M), f32),
        jnp.zeros((bsz, H_M, 1, LANES), f32), p["mlstm_norm"][l].reshape(1, DV_M), bsz, seq)
    h = h.reshape(t, D_BRANCH)

    merged = _merge_prompt(y_conv, o, h, z_g, p["w_branch"], l)
    x1, x1_bf = _outproj_ln(merged, p["w_out"], l, x, p["ln_g"][l, 0].reshape(1, -1),
                            p["ln_b"][l, 0].reshape(1, -1), alpha)

    w_r, b_r = p["router"][l]
    wts, ids = _router(x1_bf, w_r, b_r)
    row_src, row_w, pos, tile_e, tile_first, n_used = _dispatch_plan(ids[:, :2], wts[:, :2], moe_te)
    xs = _gather_rows(x1, row_src, moe_te)
    ys = _expert_mlp(xs, row_w, tile_e, tile_first, n_used, p["w_gate"], p["w_up"], p["w_down"], l, moe_te)
    x2, x2_bf = _combine_ln(x1, ys, pos, p["ln_g"][l, 1].reshape(1, -1), p["ln_b"][l, 1].reshape(1, -1), alpha)

    state = (a_k.reshape(bsz, seq, H_DA, 2 * DK_DA), a_v.reshape(bsz, seq, H_DA, DV_DA), conv_buf,
             jnp.swapaxes(ct1, -1, -2), n1.reshape(bsz, H_M, DQK_M), m1[:, :, 0, 0])
    return x2, x2_bf, state


def _sample_layer(l, x, conv_state, c0, n0, m0, cache_k, cache_v, page_table, dec_bias, p):
    f32 = F32
    alpha = (2.0 * p["depth"]) ** 0.25
    lam_init = 0.8 - 0.6 * math.exp(-0.3 * l)
    r, d = x.shape
    z = _small_mm(x, p["w_in_t"], (l,), p["b_in"][l].reshape(1, -1), transposed=True)
    seg = lambda o, n: z[:, o:o + n]
    y_conv, u = _small_call(
        _sample_conv_fn, [jax.ShapeDtypeStruct((r, D_BRANCH), f32)] * 2,
        seg(OFF_CONV, D_BRANCH), seg(OFF_CONV + D_BRANCH, D_BRANCH), seg(OFF_CONV + 2 * D_BRANCH, D_BRANCH),
        conv_state[:, 0], conv_state[:, 1], p["conv_w"][l], name="sample_conv")
    new_buf = jnp.stack([conv_state[:, 1], u], axis=1)

    a_q = seg(OFF_AQ, D_BRANCH).reshape(r, H_DA, DV_DA)
    a_k = seg(OFF_AK, D_BRANCH).reshape(r, H_DA, DV_DA)
    a_v = seg(OFF_AV, D_BRANCH).reshape(r, H_DA, DV_DA)
    bias_rows, bias_self = dec_bias
    o = _paged_attn(a_q, a_k, a_v, cache_k, cache_v, page_table, l, bias_rows, bias_self,
                    p["attn_lambda"][l], p["attn_subln"][l].reshape(1, DV_DA), lam_init).reshape(r, D_BRANCH)

    lanes = lambda a: jnp.broadcast_to(a.reshape(r, H_M, 1, 1), (r, H_M, 1, LANES))
    h, c1, n1, m1 = _sample_mlstm(
        seg(OFF_MQ, H_M * DQK_M).reshape(r, H_M, 1, DQK_M), seg(OFF_MK, H_M * DQK_M).reshape(r, H_M, 1, DQK_M),
        seg(OFF_MV, D_BRANCH).reshape(r, H_M, 1, DV_M), seg(OFF_MO, D_BRANCH).reshape(r, H_M, 1, DV_M),
        lanes(seg(OFF_MI, H_M)), lanes(seg(OFF_MF, H_M)),
        c0, n0.reshape(r, H_M, 1, DQK_M), lanes(m0), p["mlstm_norm"][l].reshape(1, DV_M))
    h = h.reshape(r, D_BRANCH)

    projs = [_small_mm(br, p["w_branch"], (l, n)) for n, br in enumerate((y_conv, o, h))]
    g = seg(OFF_G, N_BRANCH * d)

    def merge_fn(g0, g1, g2, p0, p1, p2):
        return (_sigmoid(g0) * p0 + _sigmoid(g1) * p1 + _sigmoid(g2) * p2,)

    merged, = _small_call(merge_fn, [jax.ShapeDtypeStruct((r, d), f32)],
                          g[:, :d], g[:, d:2 * d], g[:, 2 * d:], *projs, name="sample_merge")
    mix = _small_mm(merged, p["w_out"], (l,))

    def ln_fn(xr, yr, gg, bb):
        return (_layer_norm_rows(alpha * xr + yr, gg, bb),)

    x1, = _small_call(ln_fn, [jax.ShapeDtypeStruct((r, d), f32)], x, mix,
                      p["ln_g"][l, 0].reshape(1, -1), p["ln_b"][l, 0].reshape(1, -1), name="sample_ln")
    w_r, b_r = p["router"][l]
    zr = _small_mm(x1, w_r, (), b_r, tn=LANES)
    wts, ids = _small_call(_route, [jax.ShapeDtypeStruct((r, LANES), f32), jax.ShapeDtypeStruct((r, LANES), jnp.int32)],
                           zr, name="sample_route")
    combine = jnp.einsum("tk,tke->te", wts[:, :2], jax.nn.one_hot(ids[:, :2], N_EXPERTS, dtype=f32))
    y = _sample_moe(x1, combine.T.reshape(N_EXPERTS, r, 1), p["w_gate"], p["w_up"], p["w_down"], l)
    x2, = _small_call(ln_fn, [jax.ShapeDtypeStruct((r, d), f32)], x1, y,
                      p["ln_g"][l, 1].reshape(1, -1), p["ln_b"][l, 1].reshape(1, -1), name="sample_ln")
    state = (a_k.reshape(r, 1, H_DA, 2 * DK_DA), a_v.reshape(r, 1, H_DA, DV_DA), new_buf,
             c1, n1.reshape(r, H_M, DQK_M), m1[:, :, 0, 0])
    return x2, state


def kernel(x_prompt, x_sample, cache_k, cache_v, state_conv, state_mlstm_c, state_mlstm_n, state_mlstm_m,
           page_table, rel_bias, w_in, b_in, conv_w, attn_lambda, attn_subln, mlstm_norm, w_branch, w_out,
           ln_g, ln_b, w_router_group, b_router_group, w_router_expert, b_router_expert, w_gate, w_up, w_down):
    depth = w_in.shape[0]
    bp, sp, d = x_prompt.shape
    bs, ss, _ = x_sample.shape
    assert ss == 1, "the sample group decodes one token per sequence"
    psz = cache_k.shape[2]
    past_len = page_table.shape[1] * psz
    assert psz >= MAX_DISTANCE, "only the last page may hold keys closer than MAX_DISTANCE"

    params = dict(
        depth=depth, rel_bias=rel_bias, b_in=b_in, conv_w=conv_w, attn_lambda=attn_lambda, attn_subln=attn_subln,
        mlstm_norm=mlstm_norm, w_branch=w_branch, w_out=w_out, ln_g=ln_g, ln_b=ln_b,
        w_gate=w_gate, w_up=w_up, w_down=w_down,
        w_in_t=jnp.swapaxes(w_in, 1, 2),
        router=[_router_weights(w_router_group[l], b_router_group[l], w_router_expert[l], b_router_expert[l])
                for l in range(depth)],
    )

    attn_t = _tile(sp, 512)
    kk = jnp.arange(attn_t, dtype=jnp.int32)[:, None]
    qq = jnp.arange(attn_t, dtype=jnp.int32)[None, :]
    assert attn_t >= MAX_DISTANCE
    sub = _t5_bucket(qq + attn_t - kk)
    diag = jnp.where(qq >= kk, _t5_bucket(qq - kk), -1)
    bias_t = _bias_tiles(rel_bias, jnp.stack([sub, diag]))

    dec_bias = _decode_bias(rel_bias, psz, past_len)

    xp = x_prompt.reshape(bp * sp, d)
    xp_bf = xp.astype(BF16)
    xs = x_sample.reshape(bs, d)
    st_p, st_s = [], []
    for l in range(depth):
        xp, xp_bf, st = _prompt_layer(l, xp, xp_bf, bp, sp, bias_t, attn_t, 256, params)
        st_p.append(st)
        xs, st = _sample_layer(l, xs, state_conv[l], state_mlstm_c[l], state_mlstm_n[l], state_mlstm_m[l],
                               cache_k, cache_v, page_table, dec_bias, params)
        st_s.append(st)

    stack = lambda sts, i: jnp.stack([s[i] for s in sts])
    return (xp.reshape(bp, sp, d), xs.reshape(bs, ss, d),
            stack(st_p, 0), stack(st_p, 1), stack(st_s, 0), stack(st_s, 1),
            stack(st_p, 2), stack(st_s, 2),
            stack(st_p, 3), stack(st_s, 3),
            stack(st_p, 4), stack(st_s, 4),
            stack(st_p, 5), stack(st_s, 5))


def _decode_bias(rel_bias, psz, past_len):
    row = jnp.arange(psz * H_DA, dtype=jnp.int32)
    bk_last = _t5_bucket(past_len - (past_len - psz + row // H_DA))
    bk = jnp.stack([jnp.full_like(row, N_BUCKETS - 1), bk_last, jnp.zeros_like(row)])
    dec = _bias_tiles(rel_bias, bk.reshape(3, 1, psz * H_DA))[:, :, 0, :]
    own_head = (row % H_DA)[None, :] == jnp.arange(H_DA, dtype=jnp.int32)[:, None]
    dec_rows = jnp.sum(jnp.where(own_head[:, None, :], dec, 0.0), axis=0)
    return dec_rows[0:2], dec[:, 2, 0:1]
```

```python
import functools
import math

import jax
import jax.numpy as jnp
import numpy as np
from jax import lax
from jax.experimental import pallas as pl
from jax.experimental.pallas import tpu as pltpu

D_BRANCH = 1024
N_BRANCH = 3
CONV_W = 3
DK_DA = 64
DV_DA = 2 * DK_DA
H_DA = D_BRANCH // DV_DA
DQK_M = 128
DV_M = 256
H_M = D_BRANCH // DV_M
N_BUCKETS = 32
MAX_DISTANCE = 128
N_GROUPS = 4
EXP_PER_GROUP = 4
N_EXPERTS = N_GROUPS * EXP_PER_GROUP
D_EXPERT = 512
M_CHUNK = 128
LN_EPS = 1e-5

OFF_CONV = 0
OFF_AQ = 3 * D_BRANCH
OFF_AK = OFF_AQ + D_BRANCH
OFF_AV = OFF_AK + D_BRANCH
OFF_MQ = OFF_AV + D_BRANCH
OFF_MK = OFF_MQ + H_M * DQK_M
OFF_MV = OFF_MK + H_M * DQK_M
OFF_MO = OFF_MV + D_BRANCH
OFF_MI = OFF_MO + D_BRANCH
OFF_MF = OFF_MI + H_M
OFF_G = OFF_MF + H_M

F32 = jnp.float32
BF16 = jnp.bfloat16
NEG = -1e30
LANES = 128
V7X_VMEM_BUDGET = 56 * 1024 * 1024
HI = lax.Precision.HIGHEST


def _cparams(sem, vmem_mb):
    return pltpu.CompilerParams(dimension_semantics=sem,
                                vmem_limit_bytes=min(vmem_mb * 1024 * 1024, V7X_VMEM_BUDGET))


def _tile(n, pref):
    t = min(pref, n)
    while n % t:
        t //= 2
    return t


def _rnd(x):
    return x.astype(BF16).astype(F32)


def _rnd_bits(x):
    u = lax.bitcast_convert_type(x, jnp.uint32)
    r = (u + jnp.uint32(0x7FFF) + ((u >> 16) & jnp.uint32(1))) & jnp.uint32(0xFFFF0000)
    return lax.bitcast_convert_type(r, F32)


def _sigmoid(x):
    return 1.0 / (1.0 + jnp.exp(-x))


def _log_sigmoid(x):
    return jnp.minimum(x, 0.0) - jnp.log(1.0 + jnp.exp(-jnp.abs(x)))


def _layer_norm_rows(y, g, b):
    mu = jnp.mean(y, axis=-1, keepdims=True)
    yc = y - mu
    var = jnp.mean(yc * yc, axis=-1, keepdims=True)
    return yc * lax.rsqrt(var + LN_EPS) * g + b


def _proj_kernel(x_ref, w_ref, b_ref, o_ref, wbf_ref):
    @pl.when(pl.program_id(1) == 0)
    def _cast():
        wbf_ref[...] = w_ref[0].astype(BF16)

    acc = lax.dot_general(x_ref[...], wbf_ref[...], (((1,), (1,)), ((), ())), preferred_element_type=F32)
    o_ref[...] = (acc + b_ref[...]).astype(o_ref.dtype)


def _proj(x_bf, w_t, layer, row0, ncols, bias, out_dtype, tm=1024, tn=1024):
    m, k = x_bf.shape
    tm = _tile(m, tm)
    tn = _tile(ncols, tn)
    return pl.pallas_call(
        _proj_kernel,
        grid=(ncols // tn, m // tm),
        in_specs=[pl.BlockSpec((tm, k), lambda j, i: (i, 0)),
                  pl.BlockSpec((pl.Element(1), pl.Element(tn), pl.Element(k)),
                               lambda j, i: (layer, pl.multiple_of(row0 + j * tn, 8), 0)),
                  pl.BlockSpec((1, tn), lambda j, i: (0, j))],
        out_specs=pl.BlockSpec((tm, tn), lambda j, i: (i, j)),
        out_shape=jax.ShapeDtypeStruct((m, ncols), out_dtype),
        scratch_shapes=[pltpu.VMEM((tn, k), BF16)],
        compiler_params=_cparams(("arbitrary", "arbitrary"), 48),
        name="proj",
    )(x_bf, w_t, bias)


def _conv_kernel(zb_ref, zc_ref, zx_ref, w_ref, buf_ref, y_ref, nb_ref, carry_ref):
    s_idx = pl.program_id(2)

    @pl.when(s_idx == 0)
    def _init():
        carry_ref[...] = buf_ref[...]

    u = zc_ref[...] * zx_ref[...]
    ts = u.shape[0]
    row = lax.broadcasted_iota(jnp.int32, u.shape, 0)
    c0 = carry_ref[0:1, :]
    c1 = carry_ref[1:2, :]
    u1 = jnp.where(row == 0, c1, pltpu.roll(u, 1, 0))
    u2 = jnp.where(row == 0, c0, jnp.where(row == 1, c1, pltpu.roll(u, 2, 0)))
    w = w_ref[...]
    y = u2 * w[0:1, :]
    y = y + u1 * w[1:2, :]
    y = y + u * w[2:3, :]
    y_ref[...] = (zb_ref[...] * y).astype(y_ref.dtype)
    last = zc_ref[ts - 2:ts, :] * zx_ref[ts - 2:ts, :]
    carry_ref[...] = last
    nb_ref[...] = last


def _conv_prompt(z_conv, conv_w, buf, bsz, seq):
    c = D_BRANCH
    tc = _tile(c, 512)
    ts = _tile(seq, 1024)
    ncb = c // tc
    nsb = seq // ts
    z3 = z_conv.reshape(bsz, seq, 3 * c)
    y, nb = pl.pallas_call(
        _conv_kernel,
        grid=(bsz, ncb, nsb),
        in_specs=[pl.BlockSpec((None, ts, tc), lambda b, j, s: (b, s, j)),
                  pl.BlockSpec((None, ts, tc), lambda b, j, s: (b, s, ncb + j)),
                  pl.BlockSpec((None, ts, tc), lambda b, j, s: (b, s, 2 * ncb + j)),
                  pl.BlockSpec((CONV_W, tc), lambda b, j, s: (0, j)),
                  pl.BlockSpec((None, CONV_W - 1, tc), lambda b, j, s: (b, 0, j))],
        out_specs=[pl.BlockSpec((None, ts, tc), lambda b, j, s: (b, s, j)),
                   pl.BlockSpec((None, CONV_W - 1, tc), lambda b, j, s: (b, 0, j))],
        out_shape=[jax.ShapeDtypeStruct((bsz, seq, c), BF16),
                   jax.ShapeDtypeStruct((bsz, CONV_W - 1, c), F32)],
        scratch_shapes=[pltpu.VMEM((CONV_W - 1, tc), F32)],
        compiler_params=_cparams(("arbitrary", "arbitrary", "arbitrary"), 40),
        name="conv",
    )(z3, z3, z3, conv_w, buf)
    return y.reshape(bsz * seq, c), nb


def _t5_bucket(dist):
    dist = jnp.maximum(dist, 0)
    max_exact = N_BUCKETS // 2
    large = max_exact + (jnp.log(jnp.maximum(dist, 1).astype(F32) / max_exact)
                         / math.log(MAX_DISTANCE / max_exact) * (N_BUCKETS - max_exact)).astype(jnp.int32)
    large = jnp.minimum(large, N_BUCKETS - 1)
    return jnp.where(dist < max_exact, dist, large)


def _bias_kernel(rb_ref, bk_ref, o_ref):
    h = pl.program_id(0)
    bk = bk_ref[...]
    acc = jnp.full(bk.shape, NEG, F32)
    for b in range(N_BUCKETS):
        acc = jnp.where(bk == b, rb_ref[b, h], acc)
    o_ref[...] = acc


def _bias_tiles(rel_bias, buckets):
    nk, r, c = buckets.shape
    return pl.pallas_call(
        _bias_kernel,
        grid=(H_DA, nk),
        in_specs=[pl.BlockSpec(memory_space=pltpu.SMEM),
                  pl.BlockSpec((None, r, c), lambda h, k: (k, 0, 0))],
        out_specs=pl.BlockSpec((None, None, r, c), lambda h, k: (h, k, 0, 0)),
        out_shape=jax.ShapeDtypeStruct((H_DA, nk, r, c), F32),
        compiler_params=_cparams(("arbitrary", "arbitrary"), 16),
        name="bias_tiles",
    )(rel_bias, buckets)


def _attn_tile_kernel(rb_ref, q_ref, k_ref, vt_ref, bias_ref, lam_ref, sub_ref, o_ref,
                      kbf, m1_ref, l1_ref, m2_ref, l2_ref, acc_ref, *, lam_init, t):
    h = pl.program_id(1)
    i = pl.program_id(2)

    @pl.when(i == 0)
    def _cast_keys():
        kbf[...] = k_ref[...].astype(BF16)

    q = q_ref[...] * (DK_DA ** -0.5)
    lane = lax.broadcasted_iota(jnp.int32, q.shape, 1)
    q1 = jnp.where(lane < DK_DA, q, 0.0).astype(BF16)
    q2 = jnp.where(lane >= DK_DA, q, 0.0).astype(BF16)
    lv = lam_ref[...]
    lam = (jnp.exp(jnp.sum(lv[0:1, :] * lv[1:2, :], axis=1, keepdims=True))
           - jnp.exp(jnp.sum(lv[2:3, :] * lv[3:4, :], axis=1, keepdims=True)) + lam_init)
    bfar = rb_ref[N_BUCKETS - 1, h]
    nt = (((1,), (1,)), ((), ()))
    streams = ((q1, m1_ref, l1_ref), (q2, m2_ref, l2_ref))

    def keys(j):
        return kbf[pl.ds(pl.multiple_of(j * t, t), t), :]

    def score(kb, qb):
        return lax.dot_general(kb, qb, nt, preferred_element_type=F32)

    def stats(j, tile):
        kb = keys(j)
        for qb, m_ref, l_ref in streams:
            s = score(kb, qb)
            m_old = m_ref[...]
            if tile is None:
                m_new = jnp.maximum(m_old, jnp.max(s, axis=0, keepdims=True) + bfar)
                e = jnp.exp(s - (m_new - bfar))
            else:
                s = s + tile
                m_new = jnp.maximum(m_old, jnp.max(s, axis=0, keepdims=True))
                e = jnp.exp(s - m_new)
            l_ref[...] = jnp.exp(m_old - m_new) * l_ref[...] + jnp.sum(e, axis=0, keepdims=True)
            m_ref[...] = m_new

    def apply(j, tile):
        kb = keys(j)
        ps = []
        for qb, m_ref, l_ref in streams:
            s = score(kb, qb)
            if tile is None:
                ps.append(jnp.exp(s - (m_ref[...] - bfar)) * l_ref[...])
            else:
                ps.append(jnp.exp(s + tile - m_ref[...]) * l_ref[...])
        a = ps[0] - lam * ps[1]
        acc_ref[...] += jnp.dot(vt_ref[j], a.astype(BF16), preferred_element_type=F32)

    def sweep(fn):
        lax.fori_loop(0, jnp.maximum(i - 1, 0), lambda j, c: (fn(j, None), c)[1], 0)

        @pl.when(i >= 1)
        def _sub_diagonal():
            fn(i - 1, bias_ref[0])

        fn(i, bias_ref[1])

    m1_ref[...] = jnp.full(m1_ref.shape, NEG, F32)
    m2_ref[...] = jnp.full(m2_ref.shape, NEG, F32)
    l1_ref[...] = jnp.zeros(l1_ref.shape, F32)
    l2_ref[...] = jnp.zeros(l2_ref.shape, F32)
    sweep(stats)
    l1_ref[...] = 1.0 / l1_ref[...]
    l2_ref[...] = 1.0 / l2_ref[...]
    acc_ref[...] = jnp.zeros(acc_ref.shape, F32)
    sweep(apply)
    ot = acc_ref[...]
    ms = jnp.mean(ot * ot, axis=0, keepdims=True)
    on = ot * lax.rsqrt(ms + LN_EPS) * sub_ref[...] * (1.0 - lam_init)
    o_ref[...] = on.T.astype(o_ref.dtype)


def _attn_tiles(q, k, vt, rel_bias, bias_near, lam_l, subln_col, bsz, seq, lam_init, t):
    nq = seq // t
    return pl.pallas_call(
        functools.partial(_attn_tile_kernel, lam_init=lam_init, t=t),
        grid=(bsz, H_DA, nq),
        in_specs=[pl.BlockSpec(memory_space=pltpu.SMEM),
                  pl.BlockSpec((None, t, DV_DA), lambda b, h, i: (b, i, h)),
                  pl.BlockSpec((None, seq, DV_DA), lambda b, h, i: (b, 0, h)),
                  pl.BlockSpec((None, None, nq, DV_DA, t), lambda b, h, i: (b, h, 0, 0, 0)),
                  pl.BlockSpec((None, 2, t, t), lambda b, h, i: (h, 0, 0, 0)),
                  pl.BlockSpec((4, DK_DA), lambda b, h, i: (0, 0)),
                  pl.BlockSpec((DV_DA, 1), lambda b, h, i: (0, 0))],
        out_specs=pl.BlockSpec((None, t, DV_DA), lambda b, h, i: (b, i, h)),
        out_shape=jax.ShapeDtypeStruct((bsz, seq, D_BRANCH), BF16),
        scratch_shapes=[pltpu.VMEM((seq, DV_DA), BF16),
                        pltpu.VMEM((1, t), F32), pltpu.VMEM((1, t), F32),
                        pltpu.VMEM((1, t), F32), pltpu.VMEM((1, t), F32),
                        pltpu.VMEM((DV_DA, t), F32)],
        compiler_params=_cparams(("arbitrary", "arbitrary", "arbitrary"), 48),
        name="attn",
    )(rel_bias, q, k, vt, bias_near, lam_l, subln_col)


def _mlstm_kernel(q_ref, k_ref, v_ref, o_ref, i_ref, f_ref, ct0_ref, n0_ref, m0_ref, nrm_ref,
                  h_ref, ct_ref, n_ref, m_ref, ct_s, n_s, m_s):
    c_idx = pl.program_id(1)

    @pl.when(c_idx == 0)
    def _init():
        ct_s[...] = ct0_ref[...]
        n_s[...] = n0_ref[...]
        m_s[...] = m0_ref[...]

    L = q_ref.shape[0]
    r_io = lax.broadcasted_iota(jnp.int32, (L, L), 0)
    c_io = lax.broadcasted_iota(jnp.int32, (L, L), 1)
    upper = (r_io <= c_io).astype(F32)
    nt = (((1,), (1,)), ((), ()))
    for hh in range(H_M):
        qs = slice(hh * DQK_M, (hh + 1) * DQK_M)
        vs = slice(hh * DV_M, (hh + 1) * DV_M)
        q = q_ref[:, qs]
        k = k_ref[:, qs] * (DQK_M ** -0.5)
        v = v_ref[:, vs]
        i_row = i_ref[hh]
        logf = _log_sigmoid(f_ref[hh])
        b_row = jnp.dot(jnp.broadcast_to(logf, (8, L)), upper, precision=HI,
                        preferred_element_type=F32)[0:1, :]
        stack = jnp.where(r_io == 0, b_row, jnp.where(r_io == 1, i_row, 0.0))
        stack_t = stack.T
        b_col = stack_t[:, 0:1]
        i_col = stack_t[:, 1:2]
        d = jnp.where(r_io >= c_io, b_col - b_row + i_row, NEG)
        m_prev = m_s[hh][:, 0:1]
        inter = b_col + m_prev
        m_t = jnp.maximum(inter, jnp.max(d, axis=1, keepdims=True))
        w_inter = jnp.exp(inter - m_t)
        qb = q.astype(BF16)
        kb = k.astype(BF16)
        qk = lax.dot_general(qb, kb, nt, preferred_element_type=F32) * jnp.exp(d - m_t)
        ct = ct_s[hh]
        num = (jnp.dot(qk.astype(BF16), v.astype(BF16), preferred_element_type=F32)
               + w_inter * jnp.dot(qb, ct.astype(BF16), preferred_element_type=F32))
        n_row = n_s[hh]
        den = (jnp.sum(qk, axis=1, keepdims=True)
               + w_inter * jnp.sum(_rnd(q) * _rnd_bits(n_row), axis=1, keepdims=True))
        h = num / jnp.maximum(jnp.abs(den), jnp.exp(-m_t))
        m_new = m_t[L - 1:L, :]
        b_last = b_row[:, L - 1:L]
        w_end = jnp.exp(b_last - b_col + i_col - m_new)
        decay = jnp.exp(b_last + m_prev - m_new)
        ct_new = decay * ct + jnp.dot((w_end * k).T.astype(BF16), v.astype(BF16), preferred_element_type=F32)
        n_new = decay * n_row + jnp.sum(_rnd_bits(w_end) * _rnd(k), axis=0, keepdims=True)
        m_b = jnp.broadcast_to(m_new, (1, LANES))
        ct_s[hh] = ct_new
        n_s[hh] = n_new
        m_s[hh] = m_b
        ct_ref[hh] = ct_new
        n_ref[hh] = n_new
        m_ref[hh] = m_b
        hn = h * lax.rsqrt(jnp.mean(h * h, axis=1, keepdims=True) + LN_EPS) * nrm_ref[...]
        h_ref[:, vs] = (hn * _sigmoid(o_ref[:, vs])).astype(h_ref.dtype)


def _mlstm_prompt(mqk, mv, mo, i_rows, f_rows, ct0, n0, m0, mnorm, bsz, seq):
    L = M_CHUNK if seq % M_CHUNK == 0 else seq
    nc = seq // L
    hq = H_M * DQK_M
    st = lambda b, c: (b, 0, 0, 0)
    gate = pl.BlockSpec((None, H_M, 1, L), lambda b, c: (b, 0, 0, c))
    return pl.pallas_call(
        _mlstm_kernel,
        grid=(bsz, nc),
        in_specs=[pl.BlockSpec((None, L, hq), lambda b, c: (b, c, 0)),
                  pl.BlockSpec((None, L, hq), lambda b, c: (b, c, 1)),
                  pl.BlockSpec((None, L, D_BRANCH), lambda b, c: (b, c, 0)),
                  pl.BlockSpec((None, L, D_BRANCH), lambda b, c: (b, c, 0)),
                  gate, gate,
                  pl.BlockSpec((None, H_M, DQK_M, DV_M), st),
                  pl.BlockSpec((None, H_M, 1, DQK_M), st),
                  pl.BlockSpec((None, H_M, 1, LANES), st),
                  pl.BlockSpec((1, DV_M), lambda b, c: (0, 0))],
        out_specs=[pl.BlockSpec((None, L, D_BRANCH), lambda b, c: (b, c, 0)),
                   pl.BlockSpec((None, H_M, DQK_M, DV_M), st),
                   pl.BlockSpec((None, H_M, 1, DQK_M), st),
                   pl.BlockSpec((None, H_M, 1, LANES), st)],
        out_shape=[jax.ShapeDtypeStruct((bsz, seq, D_BRANCH), BF16),
                   jax.ShapeDtypeStruct((bsz, H_M, DQK_M, DV_M), F32),
                   jax.ShapeDtypeStruct((bsz, H_M, 1, DQK_M), F32),
                   jax.ShapeDtypeStruct((bsz, H_M, 1, LANES), F32)],
        scratch_shapes=[pltpu.VMEM((H_M, DQK_M, DV_M), F32), pltpu.VMEM((H_M, 1, DQK_M), F32),
                        pltpu.VMEM((H_M, 1, LANES), F32)],
        compiler_params=_cparams(("arbitrary", "arbitrary"), 32),
        name="mlstm",
    )(mqk, mqk, mv, mo, i_rows, f_rows, ct0, n0, m0, mnorm)


def _merge_kernel(b0_ref, b1_ref, b2_ref, g0_ref, g1_ref, g2_ref, w_ref, o_ref, wbf_ref):
    @pl.when(pl.program_id(1) == 0)
    def _cast():
        wbf_ref[...] = w_ref[...].astype(BF16)

    acc = _sigmoid(g0_ref[...]) * jnp.dot(b0_ref[...], wbf_ref[0], preferred_element_type=F32)
    acc = acc + _sigmoid(g1_ref[...]) * jnp.dot(b1_ref[...], wbf_ref[1], preferred_element_type=F32)
    acc = acc + _sigmoid(g2_ref[...]) * jnp.dot(b2_ref[...], wbf_ref[2], preferred_element_type=F32)
    o_ref[...] = acc.astype(o_ref.dtype)


def _merge_prompt(br0, br1, br2, zg, w_branch, layer, tm=512, tn=512):
    m, kb = br0.shape
    d = zg.shape[1] // N_BRANCH
    tm = _tile(m, tm)
    tn = _tile(d, tn)
    nn = d // tn
    bspec = pl.BlockSpec((tm, kb), lambda j, i: (i, 0))
    gspec = lambda n: pl.BlockSpec((tm, tn), lambda j, i: (i, n * nn + j))
    return pl.pallas_call(
        _merge_kernel,
        grid=(nn, m // tm),
        in_specs=[bspec, bspec, bspec, gspec(0), gspec(1), gspec(2),
                  pl.BlockSpec((None, N_BRANCH, kb, tn), lambda j, i: (layer, 0, 0, j))],
        out_specs=pl.BlockSpec((tm, tn), lambda j, i: (i, j)),
        out_shape=jax.ShapeDtypeStruct((m, d), BF16),
        scratch_shapes=[pltpu.VMEM((N_BRANCH, kb, tn), BF16)],
        compiler_params=_cparams(("arbitrary", "arbitrary"), 48),
        name="merge",
    )(br0, br1, br2, zg, zg, zg, w_branch)


def _outproj_ln_kernel(a_ref, w_ref, x_ref, g_ref, b_ref, o_ref, obf_ref, acc_ref, *, alpha, tn):
    j = pl.program_id(1)
    nn = acc_ref.shape[0]
    acc_ref[j] = jnp.dot(a_ref[...], w_ref[...].astype(BF16), preferred_element_type=F32)

    @pl.when(j == nn - 1)
    def _ln():
        d = nn * tn
        ys = [alpha * x_ref[:, c * tn:(c + 1) * tn] + acc_ref[c] for c in range(nn)]
        mu = sum(jnp.sum(y, axis=-1, keepdims=True) for y in ys) / d
        ys = [y - mu for y in ys]
        var = sum(jnp.sum(y * y, axis=-1, keepdims=True) for y in ys) / d
        inv = lax.rsqrt(var + LN_EPS)
        for c in range(nn):
            sl = slice(c * tn, (c + 1) * tn)
            out = ys[c] * inv * g_ref[:, sl] + b_ref[:, sl]
            o_ref[:, sl] = out
            obf_ref[:, sl] = out.astype(BF16)


def _outproj_ln(a_bf, w, layer, x, g, b, alpha, tm=512, tn=512):
    m, k = a_bf.shape
    d = x.shape[1]
    tm = _tile(m, tm)
    tn = _tile(d, tn)
    row = lambda i, j: (i, 0)
    return pl.pallas_call(
        functools.partial(_outproj_ln_kernel, alpha=alpha, tn=tn),
        grid=(m // tm, d // tn),
        in_specs=[pl.BlockSpec((tm, k), row),
                  pl.BlockSpec((None, k, tn), lambda i, j: (layer, 0, j)),
                  pl.BlockSpec((tm, d), row),
                  pl.BlockSpec((1, d), lambda i, j: (0, 0)),
                  pl.BlockSpec((1, d), lambda i, j: (0, 0))],
        out_specs=[pl.BlockSpec((tm, d), row), pl.BlockSpec((tm, d), row)],
        out_shape=[jax.ShapeDtypeStruct((m, d), F32), jax.ShapeDtypeStruct((m, d), BF16)],
        scratch_shapes=[pltpu.VMEM((d // tn, tm, tn), F32)],
        compiler_params=_cparams(("arbitrary", "arbitrary"), 48),
        name="outproj_ln",
    )(a_bf, w, x, g, b)


def _route(z):
    lane = lax.broadcasted_iota(jnp.int32, z.shape, 1)
    lane_f = lane.astype(F32)
    big = float(LANES)
    gmask = lane < N_GROUPS
    lg = jnp.where(gmask, z, NEG)
    gmax = jnp.max(lg, axis=1, keepdims=True)
    gsel = jnp.min(jnp.where(gmask & (lg == gmax), lane_f, big), axis=1, keepdims=True)
    pg_sel = 1.0 / jnp.sum(jnp.where(gmask, jnp.exp(lg - gmax), 0.0), axis=1, keepdims=True)
    lo = N_GROUPS + EXP_PER_GROUP * gsel
    emask = (lane_f >= lo) & (lane_f < lo + EXP_PER_GROUP)
    le = jnp.where(emask, z, NEG)
    emax = jnp.max(le, axis=1, keepdims=True)
    pe = jnp.where(emask, jnp.exp(le - emax), 0.0)
    pe = pe / jnp.sum(pe, axis=1, keepdims=True)
    pm = jnp.where(emask, pe, -1.0)
    p1 = jnp.max(pm, axis=1, keepdims=True)
    i1 = jnp.min(jnp.where(pm == p1, lane_f, big), axis=1, keepdims=True)
    pm2 = jnp.where(lane_f == i1, -1.0, pm)
    p2 = jnp.max(pm2, axis=1, keepdims=True)
    i2 = jnp.min(jnp.where(pm2 == p2, lane_f, big), axis=1, keepdims=True)
    tot = p1 + p2
    w1 = pg_sel * p1 / tot
    w2 = pg_sel * p2 / tot
    wts = jnp.where(lane == 0, w1, jnp.where(lane == 1, w2, 0.0))
    ids = jnp.where(lane == 0, i1 - N_GROUPS, jnp.where(lane == 1, i2 - N_GROUPS, 0.0)).astype(jnp.int32)
    return wts, ids


def _router_kernel(x_ref, w_ref, b_ref, wt_ref, id_ref):
    z = jnp.dot(x_ref[...], w_ref[...].astype(BF16), preferred_element_type=F32) + b_ref[...]
    wts, ids = _route(z)
    wt_ref[...] = wts
    id_ref[...] = ids


def _router(x, w_r, b_r, tm=256):
    m, k = x.shape
    tm = _tile(m, tm)
    return pl.pallas_call(
        _router_kernel,
        grid=(m // tm,),
        in_specs=[pl.BlockSpec((tm, k), lambda i: (i, 0)),
                  pl.BlockSpec((k, LANES), lambda i: (0, 0)),
                  pl.BlockSpec((1, LANES), lambda i: (0, 0))],
        out_specs=[pl.BlockSpec((tm, LANES), lambda i: (i, 0)), pl.BlockSpec((tm, LANES), lambda i: (i, 0))],
        out_shape=[jax.ShapeDtypeStruct((m, LANES), F32), jax.ShapeDtypeStruct((m, LANES), jnp.int32)],
        compiler_params=_cparams(("arbitrary",), 32),
        name="router",
    )(x, w_r, b_r)


def _router_weights(w_rg, b_rg, w_re, b_re):
    k = w_rg.shape[0]
    w = jnp.zeros((k, LANES), F32).at[:, :N_GROUPS].set(w_rg).at[:, N_GROUPS:N_GROUPS + N_EXPERTS].set(w_re)
    b = jnp.zeros((1, LANES), F32).at[0, :N_GROUPS].set(b_rg).at[0, N_GROUPS:N_GROUPS + N_EXPERTS].set(b_re)
    return w, b


def _row_copy(src_hbm, row, dst_ref, r, sem):
    return pltpu.make_async_copy(src_hbm.at[pl.ds(row, 1)], dst_ref.at[pl.ds(r, 1)], sem)


def _gather_kernel(src_ref, x_hbm, o_ref, sem):
    i = pl.program_id(0)
    te = o_ref.shape[0]

    def issue(r, c):
        _row_copy(x_hbm, src_ref[i * te + r], o_ref, r, sem).start()
        return c

    lax.fori_loop(0, te, issue, 0)

    def wait(r, c):
        _row_copy(x_hbm, 0, o_ref, r, sem).wait()
        return c

    lax.fori_loop(0, te, wait, 0)


def _gather_rows(x, row_src, te):
    nr = row_src.shape[0]
    d = x.shape[1]
    grid_spec = pltpu.PrefetchScalarGridSpec(
        num_scalar_prefetch=1,
        grid=(nr // te,),
        in_specs=[pl.BlockSpec(memory_space=pl.ANY)],
        out_specs=pl.BlockSpec((te, d), lambda i, s: (i, 0)),
        scratch_shapes=[pltpu.SemaphoreType.DMA(())])
    return pl.pallas_call(
        _gather_kernel,
        grid_spec=grid_spec,
        out_shape=jax.ShapeDtypeStruct((nr, d), x.dtype),
        compiler_params=_cparams(("arbitrary",), 32),
        name="moe_gather",
    )(row_src, x)


def _expert_kernel(te_ref, first_ref, nused_ref, xs_ref, rw_ref, wg_ref, wu_ref, wd_ref, y_ref,
                   wg_bf, wu_bf, wd_bf):
    i = pl.program_id(0)

    @pl.when(first_ref[i] == 1)
    def _cast():
        wg_bf[...] = wg_ref[...].astype(BF16)
        wu_bf[...] = wu_ref[...].astype(BF16)
        wd_bf[...] = wd_ref[...].astype(BF16)

    @pl.when(i < nused_ref[0])
    def _compute():
        xb = xs_ref[...].astype(BF16)
        g = jnp.dot(xb, wg_bf[...], preferred_element_type=F32)
        u = jnp.dot(xb, wu_bf[...], preferred_element_type=F32)
        hid = g * _sigmoid(g) * u * rw_ref[...]
        y_ref[...] = jnp.dot(hid.astype(BF16), wd_bf[...], preferred_element_type=F32)

    @pl.when(i >= nused_ref[0])
    def _pad():
        y_ref[...] = jnp.zeros(y_ref.shape, y_ref.dtype)


def _expert_mlp(xs, row_w, tile_expert, tile_first, n_used, w_gate, w_up, w_down, layer, te):
    nr, d = xs.shape
    f = w_gate.shape[-1]
    grid_spec = pltpu.PrefetchScalarGridSpec(
        num_scalar_prefetch=3,
        grid=(nr // te,),
        in_specs=[pl.BlockSpec((te, d), lambda i, e, fi, nu: (i, 0)),
                  pl.BlockSpec((te, 1), lambda i, e, fi, nu: (i, 0)),
                  pl.BlockSpec((None, None, d, f), lambda i, e, fi, nu: (layer, e[i], 0, 0)),
                  pl.BlockSpec((None, None, d, f), lambda i, e, fi, nu: (layer, e[i], 0, 0)),
                  pl.BlockSpec((None, None, f, d), lambda i, e, fi, nu: (layer, e[i], 0, 0))],
        out_specs=pl.BlockSpec((te, d), lambda i, e, fi, nu: (i, 0)),
        scratch_shapes=[pltpu.VMEM((d, f), BF16), pltpu.VMEM((d, f), BF16), pltpu.VMEM((f, d), BF16)])
    return pl.pallas_call(
        _expert_kernel,
        grid_spec=grid_spec,
        out_shape=jax.ShapeDtypeStruct((nr, d), F32),
        compiler_params=_cparams(("arbitrary",), 52),
        name="moe_experts",
    )(tile_expert, tile_first, n_used, xs, row_w, w_gate, w_up, w_down)


def _combine_ln_kernel(pos_ref, x_ref, ys_hbm, g_ref, b_ref, o_ref, obf_ref, buf0, buf1, sem, *, alpha):
    i = pl.program_id(0)
    tm = x_ref.shape[0]

    def issue(r, c):
        t = i * tm + r
        _row_copy(ys_hbm, pos_ref[2 * t], buf0, r, sem.at[0]).start()
        _row_copy(ys_hbm, pos_ref[2 * t + 1], buf1, r, sem.at[1]).start()
        return c

    lax.fori_loop(0, tm, issue, 0)

    def wait(r, c):
        _row_copy(ys_hbm, 0, buf0, r, sem.at[0]).wait()
        _row_copy(ys_hbm, 0, buf1, r, sem.at[1]).wait()
        return c

    lax.fori_loop(0, tm, wait, 0)
    y = alpha * x_ref[...] + (buf0[...] + buf1[...])
    out = _layer_norm_rows(y, g_ref[...], b_ref[...])
    o_ref[...] = out
    obf_ref[...] = out.astype(BF16)


def _combine_ln(x, ys, pos, g, b, alpha, tm=256):
    m, d = x.shape
    tm = _tile(m, tm)
    grid_spec = pltpu.PrefetchScalarGridSpec(
        num_scalar_prefetch=1,
        grid=(m // tm,),
        in_specs=[pl.BlockSpec((tm, d), lambda i, p: (i, 0)),
                  pl.BlockSpec(memory_space=pl.ANY),
                  pl.BlockSpec((1, d), lambda i, p: (0, 0)),
                  pl.BlockSpec((1, d), lambda i, p: (0, 0))],
        out_specs=[pl.BlockSpec((tm, d), lambda i, p: (i, 0)), pl.BlockSpec((tm, d), lambda i, p: (i, 0))],
        scratch_shapes=[pltpu.VMEM((tm, d), F32), pltpu.VMEM((tm, d), F32), pltpu.SemaphoreType.DMA((2,))])
    return pl.pallas_call(
        functools.partial(_combine_ln_kernel, alpha=alpha),
        grid_spec=grid_spec,
        out_shape=[jax.ShapeDtypeStruct((m, d), F32), jax.ShapeDtypeStruct((m, d), BF16)],
        compiler_params=_cparams(("arbitrary",), 40),
        name="moe_combine_ln",
    )(pos, x, ys, g, b)


def _dispatch_plan(ids, wts, te):
    t = ids.shape[0]
    a = 2 * t
    nr = a + N_EXPERTS * te
    e_flat = ids.reshape(a)
    w_flat = wts.reshape(a)
    order = jnp.argsort(e_flat, stable=True).astype(jnp.int32)
    onehot = jax.nn.one_hot(e_flat, N_EXPERTS, dtype=jnp.int32)
    counts = jnp.sum(onehot, axis=0)
    padded = ((counts + te - 1) // te) * te
    pad_end = jnp.cumsum(padded)
    pad_off = pad_end - padded
    off = jnp.cumsum(counts) - counts
    rank = jnp.sum((jnp.cumsum(onehot, axis=0) - onehot) * onehot, axis=1)
    pos = (pad_off[e_flat] + rank).astype(jnp.int32)
    n_used = (pad_end[-1] // te).astype(jnp.int32)
    tile_start = jnp.arange(nr // te, dtype=jnp.int32) * te
    tile_expert = jnp.sum((pad_end[None, :] <= tile_start[:, None]).astype(jnp.int32), axis=1)
    rows = jnp.arange(nr, dtype=jnp.int32)
    e_row = jnp.minimum(jnp.repeat(tile_expert, te), N_EXPERTS - 1)
    local = rows - pad_off[e_row]
    valid = (rows < pad_end[-1]) & (local < counts[e_row])
    src = order[jnp.clip(off[e_row] + local, 0, a - 1)]
    row_src = jnp.where(valid, src // 2, 0).astype(jnp.int32)
    row_w = jnp.where(valid, w_flat[src], 0.0).reshape(nr, 1)
    last_e = jnp.max(e_flat).astype(jnp.int32)
    tile_expert = jnp.minimum(tile_expert, last_e).astype(jnp.int32)
    prev = jnp.concatenate([jnp.full((1,), -1, jnp.int32), tile_expert[:-1]])
    tile_first = (tile_expert != prev).astype(jnp.int32)
    return row_src, row_w, pos, tile_expert, tile_first, n_used.reshape(1)


def _small_mm_kernel(x_ref, w_ref, b_ref, o_ref):
    o_ref[...] = jnp.dot(x_ref[...].astype(BF16), w_ref[...].astype(BF16), preferred_element_type=F32) + b_ref[...]


def _small_mm_t_kernel(x_ref, w_ref, b_ref, o_ref):
    o_ref[...] = lax.dot_general(x_ref[...].astype(BF16), w_ref[...].astype(BF16), (((1,), (1,)), ((), ())),
                                 preferred_element_type=F32) + b_ref[...]


def _small_mm(x, w, lead, bias=None, tn=1024, transposed=False):
    r, k = x.shape
    n = w.shape[-2] if transposed else w.shape[-1]
    tn = min(tn, n)
    nlead = len(lead)
    if bias is None:
        bias = jnp.zeros((1, n), F32)
    if transposed:
        w_spec = pl.BlockSpec((None,) * nlead + (tn, k), lambda j: tuple(lead) + (j, 0))
    else:
        w_spec = pl.BlockSpec((None,) * nlead + (k, tn), lambda j: tuple(lead) + (0, j))
    return pl.pallas_call(
        _small_mm_t_kernel if transposed else _small_mm_kernel,
        grid=(pl.cdiv(n, tn),),
        in_specs=[pl.BlockSpec((r, k), lambda j: (0, 0)),
                  w_spec,
                  pl.BlockSpec((1, tn), lambda j: (0, j))],
        out_specs=pl.BlockSpec((r, tn), lambda j: (0, j)),
        out_shape=jax.ShapeDtypeStruct((r, n), F32),
        compiler_params=_cparams(("arbitrary",), 40),
        name="small_mm",
    )(x, w, bias)


def _small_call(fn, out_shapes, *args, name="small"):
    n_in = len(args)

    def kern(*refs):
        outs = fn(*[r[...] for r in refs[:n_in]])
        for r, o in zip(refs[n_in:], outs):
            r[...] = o

    return pl.pallas_call(kern, out_shape=out_shapes, name=name)(*args)


def _sample_conv_fn(zb, zc, zx, buf0, buf1, w):
    u = zc * zx
    y = buf0 * w[0:1, :]
    y = y + buf1 * w[1:2, :]
    y = y + u * w[2:3, :]
    return zb * y, u


def _sample_mlstm_kernel(q_ref, k_ref, v_ref, vc_ref, o_ref, i_ref, f_ref, c_ref, n_ref, m_ref, nrm_ref,
                         h_ref, c_out, n_out, m_out):
    for hh in range(H_M):
        q = q_ref[hh]
        k = k_ref[hh] * (DQK_M ** -0.5)
        v = v_ref[hh]
        v_col = vc_ref[hh]
        ig = i_ref[hh][:, 0:1]
        logf = _log_sigmoid(f_ref[hh][:, 0:1])
        m_prev = m_ref[hh][:, 0:1]
        c = c_ref[hh]
        n = n_ref[hh]
        inter = logf + m_prev
        m_t = jnp.maximum(inter, ig)
        w_inter = jnp.exp(inter - m_t)
        qk = jnp.sum(q * k, axis=1, keepdims=True) * jnp.exp(ig - m_t)
        cq = lax.dot_general(jnp.broadcast_to(q, (8, DQK_M)).astype(BF16), c.astype(BF16),
                             (((1,), (1,)), ((), ())), preferred_element_type=F32)[0:1, :]
        num = qk * v + w_inter * cq
        den = qk + w_inter * jnp.sum(n * q, axis=1, keepdims=True)
        h = num / jnp.maximum(jnp.abs(den), jnp.exp(-m_t))
        w_end = jnp.exp(ig - m_t)
        decay = jnp.exp(logf + m_prev - m_t)
        c_out[hh] = decay * c + w_end * (v_col * k)
        n_out[hh] = decay * n + w_end * k
        m_out[hh] = jnp.broadcast_to(m_t, (1, LANES))
        hn = h * lax.rsqrt(jnp.mean(h * h, axis=1, keepdims=True) + LN_EPS) * nrm_ref[...]
        h_ref[hh] = hn * _sigmoid(o_ref[hh])


def _sample_mlstm(mq, mk, mv, mo, mi, mf, c0, n0, m0, mnorm):
    r = mq.shape[0]
    mv_col = mv.reshape(r, H_M, DV_M, 1)
    blk = lambda *s: pl.BlockSpec((None,) + s, lambda b: (b,) + (0,) * len(s))
    return pl.pallas_call(
        _sample_mlstm_kernel,
        grid=(r,),
        in_specs=[blk(H_M, 1, DQK_M), blk(H_M, 1, DQK_M), blk(H_M, 1, DV_M), blk(H_M, DV_M, 1), blk(H_M, 1, DV_M),
                  blk(H_M, 1, LANES), blk(H_M, 1, LANES), blk(H_M, DV_M, DQK_M), blk(H_M, 1, DQK_M),
                  blk(H_M, 1, LANES), pl.BlockSpec((1, DV_M), lambda b: (0, 0))],
        out_specs=[blk(H_M, 1, DV_M), blk(H_M, DV_M, DQK_M), blk(H_M, 1, DQK_M), blk(H_M, 1, LANES)],
        out_shape=[jax.ShapeDtypeStruct((r, H_M, 1, DV_M), F32),
                   jax.ShapeDtypeStruct((r, H_M, DV_M, DQK_M), F32),
                   jax.ShapeDtypeStruct((r, H_M, 1, DQK_M), F32),
                   jax.ShapeDtypeStruct((r, H_M, 1, LANES), F32)],
        compiler_params=_cparams(("arbitrary",), 32),
        name="sample_mlstm",
    )(mq, mk, mv, mv_col, mo, mi, mf, c0, n0, m0, mnorm)


def _class_reduce(x, op):
    sh = H_DA
    while sh < x.shape[-1]:
        x = op(x, pltpu.roll(x, sh, x.ndim - 1))
        sh *= 2
    return x


def _paged_attn_kernel(pt_ref, q_ref, kn_ref, vn_ref, *rest, lam_init, group):
    k_refs, v_refs = rest[:group], rest[group:2 * group]
    (brow_ref, bself_ref, lam_ref, sub_ref, o_ref, qsel, sc, m_run, l_run, mb, ilb, acc, lam_s) = rest[2 * group:]
    sweep = pl.program_id(1)
    p = pl.program_id(2)
    last_step = pl.num_programs(2) - 1
    rows = k_refs[0].shape[0]
    nt = (((1,), (1,)), ((), ()))
    r_io = lax.broadcasted_iota(jnp.int32, (2 * H_DA, rows), 0)
    c_io = lax.broadcasted_iota(jnp.int32, (2 * H_DA, rows), 1)
    own = (c_io % H_DA) == (r_io % H_DA)
    lane = lax.broadcasted_iota(jnp.int32, (H_DA, DV_DA), 1)
    first = lane < DK_DA

    def to_lanes(col):
        return jnp.sum(jnp.where(own[0:H_DA], col, 0.0), axis=0, keepdims=True)

    def to_col(row):
        return jnp.max(jnp.where(own[0:H_DA], row, NEG), axis=1, keepdims=True)

    def self_scores():
        prod = _rnd_bits(kn_ref[...]) * _rnd_bits(q_ref[...] * (DK_DA ** -0.5))
        t1 = jnp.sum(jnp.where(first, prod, 0.0), axis=1, keepdims=True) + bself_ref[...]
        t2 = jnp.sum(jnp.where(first, 0.0, prod), axis=1, keepdims=True) + bself_ref[...]
        return t1, t2

    @pl.when((sweep == 0) & (p == 0))
    def _init():
        q = q_ref[...] * (DK_DA ** -0.5)
        qsel[...] = jnp.concatenate([jnp.where(first, q, 0.0), jnp.where(first, 0.0, q)], axis=0).astype(BF16)
        m_run[...] = jnp.full(m_run.shape, NEG, F32)
        l_run[...] = jnp.zeros(l_run.shape, F32)
        lv = lam_ref[...]
        lam = (jnp.exp(jnp.sum(lv[0:1, :] * lv[1:2, :], axis=1, keepdims=True))
               - jnp.exp(jnp.sum(lv[2:3, :] * lv[3:4, :], axis=1, keepdims=True)) + lam_init)
        lam_s[...] = jnp.broadcast_to(lam, lam_s.shape)

    @pl.when(sweep == 0)
    def _scores():
        for i, k_ref in enumerate(k_refs):
            r = lax.dot_general(qsel[...], k_ref[...].astype(BF16), nt, preferred_element_type=F32)
            r = jnp.where(own, r, 0.0)
            if i == group - 1:
                bias = jnp.where(p == last_step, brow_ref[1:2, :], brow_ref[0:1, :])
            else:
                bias = brow_ref[0:1, :]
            for c in range(2):
                s = jnp.sum(r[c * H_DA:(c + 1) * H_DA], axis=0, keepdims=True) + bias
                sc[p * group + i, c:c + 1, :] = s
                m_old = m_run[c:c + 1, :]
                m_new = jnp.maximum(m_old, s)
                l_run[c:c + 1, :] = l_run[c:c + 1, :] * jnp.exp(m_old - m_new) + jnp.exp(s - m_new)
                m_run[c:c + 1, :] = m_new

        @pl.when(p == last_step)
        def _close():
            for c, t in enumerate(self_scores()):
                tb = to_lanes(t)
                m_c = m_run[c:c + 1, :]
                m_all = jnp.maximum(_class_reduce(m_c, jnp.maximum), tb)
                l_all = _class_reduce(l_run[c:c + 1, :] * jnp.exp(m_c - m_all), jnp.add) + jnp.exp(tb - m_all)
                mb[c:c + 1, :] = m_all
                ilb[c:c + 1, :] = 1.0 / l_all
            acc[...] = jnp.zeros(acc.shape, F32)

    @pl.when(sweep == 1)
    def _apply():
        lam = lam_s[:, 0:1]
        part = jnp.zeros(acc.shape, F32)
        for i, v_ref in enumerate(v_refs):
            pg = p * group + i
            w1 = jnp.exp(sc[pg, 0:1, :] - mb[0:1, :]) * ilb[0:1, :]
            w2 = jnp.exp(sc[pg, 1:2, :] - mb[1:2, :]) * ilb[1:2, :]
            a = jnp.where(own, w1 - lam * w2, 0.0).astype(BF16)
            part = part + jnp.dot(a, v_ref[...].astype(BF16), preferred_element_type=F32)
        acc[...] += part

        @pl.when(p == last_step)
        def _fin():
            t1, t2 = self_scores()
            w1s = jnp.exp(t1 - to_col(mb[0:1, :])) * to_col(ilb[0:1, :])
            w2s = jnp.exp(t2 - to_col(mb[1:2, :])) * to_col(ilb[1:2, :])
            o = acc[0:H_DA, :] + _rnd_bits(w1s - lam * w2s) * _rnd_bits(vn_ref[...])
            on = o * lax.rsqrt(jnp.mean(o * o, axis=1, keepdims=True) + LN_EPS) * sub_ref[...] * (1.0 - lam_init)
            o_ref[...] = on


def _paged_attn(q, k_new, v_new, cache_k, cache_v, page_table, layer, bias_rows, bias_self, lam_l, subln_row,
                lam_init):
    r = q.shape[0]
    npg = page_table.shape[1]
    psz = cache_k.shape[2]
    rows = psz * H_DA
    pt = page_table.reshape(-1)
    tok = lambda b, s, p, t: (b, 0, 0)
    cst2 = lambda b, s, p, t: (0, 0)
    group = _tile(npg, 8)
    nsteps = npg // group

    def kpage(i):
        return lambda b, s, p, t: (layer, t[b * npg + (p * (1 - s) + (nsteps - 1) * s) * group + i], 0, 0)

    def vpage(i):
        return lambda b, s, p, t: (layer, t[b * npg + p * s * group + i], 0, 0)

    pool = cache_k.shape[1]
    cache_k = cache_k.reshape(cache_k.shape[0], pool, rows, DV_DA)
    cache_v = cache_v.reshape(cache_v.shape[0], pool, rows, DV_DA)
    grid_spec = pltpu.PrefetchScalarGridSpec(
        num_scalar_prefetch=1,
        grid=(r, 2, nsteps),
        in_specs=[pl.BlockSpec((None, H_DA, DV_DA), tok),
                  pl.BlockSpec((None, H_DA, DV_DA), tok),
                  pl.BlockSpec((None, H_DA, DV_DA), tok)]
                 + [pl.BlockSpec((None, None, rows, DV_DA), kpage(i)) for i in range(group)]
                 + [pl.BlockSpec((None, None, rows, DV_DA), vpage(i)) for i in range(group)]
                 + [pl.BlockSpec((2, rows), cst2),
                  pl.BlockSpec((H_DA, 1), cst2),
                  pl.BlockSpec((4, DK_DA), cst2),
                  pl.BlockSpec((1, DV_DA), cst2)],
        out_specs=pl.BlockSpec((None, H_DA, DV_DA), tok),
        scratch_shapes=[pltpu.VMEM((2 * H_DA, DV_DA), BF16), pltpu.VMEM((npg, 2, rows), F32),
                        pltpu.VMEM((2, rows), F32), pltpu.VMEM((2, rows), F32),
                        pltpu.VMEM((2, rows), F32), pltpu.VMEM((2, rows), F32),
                        pltpu.VMEM((2 * H_DA, DV_DA), F32), pltpu.VMEM((1, LANES), F32)])
    return pl.pallas_call(
        functools.partial(_paged_attn_kernel, lam_init=lam_init, group=group),
        grid_spec=grid_spec,
        out_shape=jax.ShapeDtypeStruct((r, H_DA, DV_DA), F32),
        compiler_params=_cparams(("arbitrary", "arbitrary", "arbitrary"), 40),
        name="paged_attn",
    )(pt, q, k_new, v_new, *([cache_k] * group), *([cache_v] * group), bias_rows, bias_self, lam_l, subln_row)


def _sample_moe_kernel(x_ref, cmb_ref, wg_ref, wu_ref, wd_ref, y_ref):
    e = pl.program_id(0)

    @pl.when(e == 0)
    def _init():
        y_ref[...] = jnp.zeros(y_ref.shape, F32)

    x = x_ref[...].astype(BF16)
    g = jnp.dot(x, wg_ref[...].astype(BF16), preferred_element_type=F32)
    u = jnp.dot(x, wu_ref[...].astype(BF16), preferred_element_type=F32)
    hid = g * _sigmoid(g) * u * cmb_ref[...]
    y_ref[...] += jnp.dot(hid.astype(BF16), wd_ref[...].astype(BF16), preferred_element_type=F32)


def _sample_moe(x, combine_t, w_gate, w_up, w_down, layer):
    r, d = x.shape
    f = w_gate.shape[-1]
    return pl.pallas_call(
        _sample_moe_kernel,
        grid=(N_EXPERTS,),
        in_specs=[pl.BlockSpec((r, d), lambda e: (0, 0)),
                  pl.BlockSpec((None, r, 1), lambda e: (e, 0, 0)),
                  pl.BlockSpec((None, None, d, f), lambda e: (layer, e, 0, 0)),
                  pl.BlockSpec((None, None, d, f), lambda e: (layer, e, 0, 0)),
                  pl.BlockSpec((None, None, f, d), lambda e: (layer, e, 0, 0))],
        out_specs=pl.BlockSpec((r, d), lambda e: (0, 0)),
        out_shape=jax.ShapeDtypeStruct((r, d), F32),
        compiler_params=_cparams(("arbitrary",), 48),
        name="sample_moe",
    )(x, combine_t, w_gate, w_up, w_down)


def _prompt_layer(l, x, x_bf, bsz, seq, bias_t, attn_t, moe_te, p):
    f32 = F32
    alpha = (2.0 * p["depth"]) ** 0.25
    lam_init = 0.8 - 0.6 * math.exp(-0.3 * l)
    t = bsz * seq
    w_in_t, b_in = p["w_in_t"], p["b_in"]
    bseg = lambda o, n: b_in[l, o:o + n].reshape(1, n)
    pr = lambda o, n, dt=f32: _proj(x_bf, w_in_t, l, o, n, bseg(o, n), dt)
    z_conv = pr(OFF_CONV, 3 * D_BRANCH)
    a_q = pr(OFF_AQ, D_BRANCH)
    a_k = pr(OFF_AK, D_BRANCH)
    a_v = pr(OFF_AV, D_BRANCH)
    m_qk = pr(OFF_MQ, 2 * H_M * DQK_M)
    m_v = pr(OFF_MV, D_BRANCH)
    m_o = pr(OFF_MO, D_BRANCH)
    z_g = pr(OFF_G, N_BRANCH * x.shape[1])
    z_if = pr(OFF_MI, LANES)

    y_conv, conv_buf = _conv_prompt(z_conv, p["conv_w"][l], jnp.zeros((bsz, CONV_W - 1, D_BRANCH), f32), bsz, seq)

    vt = a_v.reshape(bsz, seq // attn_t, attn_t, H_DA, DV_DA).transpose(0, 3, 1, 4, 2).astype(BF16)
    o = _attn_tiles(a_q.reshape(bsz, seq, D_BRANCH), a_k.reshape(bsz, seq, D_BRANCH), vt, p["rel_bias"], bias_t,
                    p["attn_lambda"][l], p["attn_subln"][l].reshape(DV_DA, 1), bsz, seq, lam_init, attn_t)
    o = o.reshape(t, D_BRANCH)

    gates = z_if[:, :2 * H_M].reshape(bsz, seq, 2, H_M).transpose(2, 0, 3, 1).reshape(2, bsz, H_M, 1, seq)
    h, ct1, n1, m1 = _mlstm_prompt(
        m_qk.reshape(bsz, seq, D_BRANCH), m_v.reshape(bsz, seq, D_BRANCH), m_o.reshape(bsz, seq, D_BRANCH),
        gates[0], gates[1],
        jnp.zeros((bsz, H_M, DQK_M, DV_M), f32), jnp.zeros((bsz, H_M, 1, DQK_M), f32),
        jnp.zeros((bsz, H_M, 1, LANES), f32), p["mlstm_norm"][l].reshape(1, DV_M), bsz, seq)
    h = h.reshape(t, D_BRANCH)

    merged = _merge_prompt(y_conv, o, h, z_g, p["w_branch"], l)
    x1, x1_bf = _outproj_ln(merged, p["w_out"], l, x, p["ln_g"][l, 0].reshape(1, -1),
                            p["ln_b"][l, 0].reshape(1, -1), alpha)

    w_r, b_r = p["router"][l]
    wts, ids = _router(x1_bf, w_r, b_r)
    row_src, row_w, pos, tile_e, tile_first, n_used = _dispatch_plan(ids[:, :2], wts[:, :2], moe_te)
    xs = _gather_rows(x1, row_src, moe_te)
    ys = _expert_mlp(xs, row_w, tile_e, tile_first, n_used, p["w_gate"], p["w_up"], p["w_down"], l, moe_te)
    x2, x2_bf = _combine_ln(x1, ys, pos, p["ln_g"][l, 1].reshape(1, -1), p["ln_b"][l, 1].reshape(1, -1), alpha)

    state = (a_k.reshape(bsz, seq, H_DA, 2 * DK_DA), a_v.reshape(bsz, seq, H_DA, DV_DA), conv_buf,
             jnp.swapaxes(ct1, -1, -2), n1.reshape(bsz, H_M, DQK_M), m1[:, :, 0, 0])
    return x2, x2_bf, state


def _sample_layer(l, x, conv_state, c0, n0, m0, cache_k, cache_v, page_table, dec_bias, p):
    f32 = F32
    alpha = (2.0 * p["depth"]) ** 0.25
    lam_init = 0.8 - 0.6 * math.exp(-0.3 * l)
    r, d = x.shape
    z = _small_mm(x, p["w_in_t"], (l,), p["b_in"][l].reshape(1, -1), transposed=True)
    seg = lambda o, n: z[:, o:o + n]
    y_conv, u = _small_call(
        _sample_conv_fn, [jax.ShapeDtypeStruct((r, D_BRANCH), f32)] * 2,
        seg(OFF_CONV, D_BRANCH), seg(OFF_CONV + D_BRANCH, D_BRANCH), seg(OFF_CONV + 2 * D_BRANCH, D_BRANCH),
        conv_state[:, 0], conv_state[:, 1], p["conv_w"][l], name="sample_conv")
    new_buf = jnp.stack([conv_state[:, 1], u], axis=1)

    a_q = seg(OFF_AQ, D_BRANCH).reshape(r, H_DA, DV_DA)
    a_k = seg(OFF_AK, D_BRANCH).reshape(r, H_DA, DV_DA)
    a_v = seg(OFF_AV, D_BRANCH).reshape(r, H_DA, DV_DA)
    bias_rows, bias_self = dec_bias
    o = _paged_attn(a_q, a_k, a_v, cache_k, cache_v, page_table, l, bias_rows, bias_self,
                    p["attn_lambda"][l], p["attn_subln"][l].reshape(1, DV_DA), lam_init).reshape(r, D_BRANCH)

    lanes = lambda a: jnp.broadcast_to(a.reshape(r, H_M, 1, 1), (r, H_M, 1, LANES))
    h, c1, n1, m1 = _sample_mlstm(
        seg(OFF_MQ, H_M * DQK_M).reshape(r, H_M, 1, DQK_M), seg(OFF_MK, H_M * DQK_M).reshape(r, H_M, 1, DQK_M),
        seg(OFF_MV, D_BRANCH).reshape(r, H_M, 1, DV_M), seg(OFF_MO, D_BRANCH).reshape(r, H_M, 1, DV_M),
        lanes(seg(OFF_MI, H_M)), lanes(seg(OFF_MF, H_M)),
        c0, n0.reshape(r, H_M, 1, DQK_M), lanes(m0), p["mlstm_norm"][l].reshape(1, DV_M))
    h = h.reshape(r, D_BRANCH)

    projs = [_small_mm(br, p["w_branch"], (l, n)) for n, br in enumerate((y_conv, o, h))]
    g = seg(OFF_G, N_BRANCH * d)

    def merge_fn(g0, g1, g2, p0, p1, p2):
        return (_sigmoid(g0) * p0 + _sigmoid(g1) * p1 + _sigmoid(g2) * p2,)

    merged, = _small_call(merge_fn, [jax.ShapeDtypeStruct((r, d), f32)],
                          g[:, :d], g[:, d:2 * d], g[:, 2 * d:], *projs, name="sample_merge")
    mix = _small_mm(merged, p["w_out"], (l,))

    def ln_fn(xr, yr, gg, bb):
        return (_layer_norm_rows(alpha * xr + yr, gg, bb),)

    x1, = _small_call(ln_fn, [jax.ShapeDtypeStruct((r, d), f32)], x, mix,
                      p["ln_g"][l, 0].reshape(1, -1), p["ln_b"][l, 0].reshape(1, -1), name="sample_ln")
    w_r, b_r = p["router"][l]
    zr = _small_mm(x1, w_r, (), b_r, tn=LANES)
    wts, ids = _small_call(_route, [jax.ShapeDtypeStruct((r, LANES), f32), jax.ShapeDtypeStruct((r, LANES), jnp.int32)],
                           zr, name="sample_route")
    combine = jnp.einsum("tk,tke->te", wts[:, :2], jax.nn.one_hot(ids[:, :2], N_EXPERTS, dtype=f32))
    y = _sample_moe(x1, combine.T.reshape(N_EXPERTS, r, 1), p["w_gate"], p["w_up"], p["w_down"], l)
    x2, = _small_call(ln_fn, [jax.ShapeDtypeStruct((r, d), f32)], x1, y,
                      p["ln_g"][l, 1].reshape(1, -1), p["ln_b"][l, 1].reshape(1, -1), name="sample_ln")
    state = (a_k.reshape(r, 1, H_DA, 2 * DK_DA), a_v.reshape(r, 1, H_DA, DV_DA), new_buf,
             c1, n1.reshape(r, H_M, DQK_M), m1[:, :, 0, 0])
    return x2, state


def kernel(x_prompt, x_sample, cache_k, cache_v, state_conv, state_mlstm_c, state_mlstm_n, state_mlstm_m,
           page_table, rel_bias, w_in, b_in, conv_w, attn_lambda, attn_subln, mlstm_norm, w_branch, w_out,
           ln_g, ln_b, w_router_group, b_router_group, w_router_expert, b_router_expert, w_gate, w_up, w_down):
    depth = w_in.shape[0]
    bp, sp, d = x_prompt.shape
    bs, ss, _ = x_sample.shape
    assert ss == 1, "the sample group decodes one token per sequence"
    psz = cache_k.shape[2]
    past_len = page_table.shape[1] * psz
    assert psz >= MAX_DISTANCE, "only the last page may hold keys closer than MAX_DISTANCE"

    params = dict(
        depth=depth, rel_bias=rel_bias, b_in=b_in, conv_w=conv_w, attn_lambda=attn_lambda, attn_subln=attn_subln,
        mlstm_norm=mlstm_norm, w_branch=w_branch, w_out=w_out, ln_g=ln_g, ln_b=ln_b,
        w_gate=w_gate, w_up=w_up, w_down=w_down,
        w_in_t=jnp.swapaxes(w_in, 1, 2),
        router=[_router_weights(w_router_group[l], b_router_group[l], w_router_expert[l], b_router_expert[l])
                for l in range(depth)],
    )

    attn_t = _tile(sp, 512)
    kk = jnp.arange(attn_t, dtype=jnp.int32)[:, None]
    qq = jnp.arange(attn_t, dtype=jnp.int32)[None, :]
    assert attn_t >= MAX_DISTANCE
    sub = _t5_bucket(qq + attn_t - kk)
    diag = jnp.where(qq >= kk, _t5_bucket(qq - kk), -1)
    bias_t = _bias_tiles(rel_bias, jnp.stack([sub, diag]))

    dec_bias = _decode_bias(rel_bias, psz, past_len)

    xp = x_prompt.reshape(bp * sp, d)
    xp_bf = xp.astype(BF16)
    xs = x_sample.reshape(bs, d)
    st_p, st_s = [], []
    for l in range(depth):
        xp, xp_bf, st = _prompt_layer(l, xp, xp_bf, bp, sp, bias_t, attn_t, 256, params)
        st_p.append(st)
        xs, st = _sample_layer(l, xs, state_conv[l], state_mlstm_c[l], state_mlstm_n[l], state_mlstm_m[l],
                               cache_k, cache_v, page_table, dec_bias, params)
        st_s.append(st)

    stack = lambda sts, i: jnp.stack([s[i] for s in sts])
    return (xp.reshape(bp, sp, d), xs.reshape(bs, ss, d),
            stack(st_p, 0), stack(st_p, 1), stack(st_s, 0), stack(st_s, 1),
            stack(st_p, 2), stack(st_s, 2),
            stack(st_p, 3), stack(st_s, 3),
            stack(st_p, 4), stack(st_s, 4),
            stack(st_p, 5), stack(st_s, 5))


def _decode_bias(rel_bias, psz, past_len):
    row = jnp.arange(psz * H_DA, dtype=jnp.int32)
    bk_last = _t5_bucket(past_len - (past_len - psz + row // H_DA))
    bk = jnp.stack([jnp.full_like(row, N_BUCKETS - 1), bk_last, jnp.zeros_like(row)])
    dec = _bias_tiles(rel_bias, bk.reshape(3, 1, psz * H_DA))[:, :, 0, :]
    own_head = (row % H_DA)[None, :] == jnp.arange(H_DA, dtype=jnp.int32)[:, None]
    dec_rows = jnp.sum(jnp.where(own_head[:, None, :], dec, 0.0), axis=0)
    return dec_rows[0:2], dec[:, 2, 0:1]
```

```python
import functools
import math

import jax
import jax.numpy as jnp
import numpy as np
from jax import lax
from jax.experimental import pallas as pl
from jax.experimental.pallas import tpu as pltpu

D_BRANCH = 1024
N_BRANCH = 3
CONV_W = 3
DK_DA = 64
DV_DA = 2 * DK_DA
H_DA = D_BRANCH // DV_DA
DQK_M = 128
DV_M = 256
H_M = D_BRANCH // DV_M
N_BUCKETS = 32
MAX_DISTANCE = 128
N_GROUPS = 4
EXP_PER_GROUP = 4
N_EXPERTS = N_GROUPS * EXP_PER_GROUP
D_EXPERT = 512
M_CHUNK = 128
LN_EPS = 1e-5

OFF_CONV = 0
OFF_AQ = 3 * D_BRANCH
OFF_AK = OFF_AQ + D_BRANCH
OFF_AV = OFF_AK + D_BRANCH
OFF_MQ = OFF_AV + D_BRANCH
OFF_MK = OFF_MQ + H_M * DQK_M
OFF_MV = OFF_MK + H_M * DQK_M
OFF_MO = OFF_MV + D_BRANCH
OFF_MI = OFF_MO + D_BRANCH
OFF_MF = OFF_MI + H_M
OFF_G = OFF_MF + H_M

F32 = jnp.float32
BF16 = jnp.bfloat16
NEG = -1e30
LANES = 128
V7X_VMEM_BUDGET = 56 * 1024 * 1024
HI = lax.Precision.HIGHEST


def _cparams(sem, vmem_mb):
    return pltpu.CompilerParams(dimension_semantics=sem,
                                vmem_limit_bytes=min(vmem_mb * 1024 * 1024, V7X_VMEM_BUDGET))


def _tile(n, pref):
    t = min(pref, n)
    while n % t:
        t //= 2
    return t


def _rnd(x):
    return x.astype(BF16).astype(F32)


def _rnd_bits(x):
    u = lax.bitcast_convert_type(x, jnp.uint32)
    r = (u + jnp.uint32(0x7FFF) + ((u >> 16) & jnp.uint32(1))) & jnp.uint32(0xFFFF0000)
    return lax.bitcast_convert_type(r, F32)


def _sigmoid(x):
    return 1.0 / (1.0 + jnp.exp(-x))


def _log_sigmoid(x):
    return jnp.minimum(x, 0.0) - jnp.log(1.0 + jnp.exp(-jnp.abs(x)))


def _layer_norm_rows(y, g, b):
    mu = jnp.mean(y, axis=-1, keepdims=True)
    yc = y - mu
    var = jnp.mean(yc * yc, axis=-1, keepdims=True)
    return yc * lax.rsqrt(var + LN_EPS) * g + b


def _proj_kernel(x_ref, w_ref, b_ref, o_ref, wbf_ref):
    @pl.when(pl.program_id(1) == 0)
    def _cast():
        wbf_ref[...] = w_ref[0].astype(BF16)

    acc = lax.dot_general(x_ref[...], wbf_ref[...], (((1,), (1,)), ((), ())), preferred_element_type=F32)
    o_ref[...] = (acc + b_ref[...]).astype(o_ref.dtype)


def _proj(x_bf, w_t, layer, row0, ncols, bias, out_dtype, tm=1024, tn=1024):
    m, k = x_bf.shape
    tm = _tile(m, tm)
    tn = _tile(ncols, tn)
    return pl.pallas_call(
        _proj_kernel,
        grid=(ncols // tn, m // tm),
        in_specs=[pl.BlockSpec((tm, k), lambda j, i: (i, 0)),
                  pl.BlockSpec((pl.Element(1), pl.Element(tn), pl.Element(k)),
                               lambda j, i: (layer, pl.multiple_of(row0 + j * tn, 8), 0)),
                  pl.BlockSpec((1, tn), lambda j, i: (0, j))],
        out_specs=pl.BlockSpec((tm, tn), lambda j, i: (i, j)),
        out_shape=jax.ShapeDtypeStruct((m, ncols), out_dtype),
        scratch_shapes=[pltpu.VMEM((tn, k), BF16)],
        compiler_params=_cparams(("arbitrary", "arbitrary"), 48),
        name="proj",
    )(x_bf, w_t, bias)


def _conv_kernel(zb_ref, zc_ref, zx_ref, w_ref, buf_ref, y_ref, nb_ref, carry_ref):
    s_idx = pl.program_id(2)

    @pl.when(s_idx == 0)
    def _init():
        carry_ref[...] = buf_ref[...]

    u = zc_ref[...] * zx_ref[...]
    ts = u.shape[0]
    row = lax.broadcasted_iota(jnp.int32, u.shape, 0)
    c0 = carry_ref[0:1, :]
    c1 = carry_ref[1:2, :]
    u1 = jnp.where(row == 0, c1, pltpu.roll(u, 1, 0))
    u2 = jnp.where(row == 0, c0, jnp.where(row == 1, c1, pltpu.roll(u, 2, 0)))
    w = w_ref[...]
    y = u2 * w[0:1, :]
    y = y + u1 * w[1:2, :]
    y = y + u * w[2:3, :]
    y_ref[...] = (zb_ref[...] * y).astype(y_ref.dtype)
    last = zc_ref[ts - 2:ts, :] * zx_ref[ts - 2:ts, :]
    carry_ref[...] = last
    nb_ref[...] = last


def _conv_prompt(z_conv, conv_w, buf, bsz, seq):
    c = D_BRANCH
    tc = _tile(c, 512)
    ts = _tile(seq, 1024)
    ncb = c // tc
    nsb = seq // ts
    z3 = z_conv.reshape(bsz, seq, 3 * c)
    y, nb = pl.pallas_call(
        _conv_kernel,
        grid=(bsz, ncb, nsb),
        in_specs=[pl.BlockSpec((None, ts, tc), lambda b, j, s: (b, s, j)),
                  pl.BlockSpec((None, ts, tc), lambda b, j, s: (b, s, ncb + j)),
                  pl.BlockSpec((None, ts, tc), lambda b, j, s: (b, s, 2 * ncb + j)),
                  pl.BlockSpec((CONV_W, tc), lambda b, j, s: (0, j)),
                  pl.BlockSpec((None, CONV_W - 1, tc), lambda b, j, s: (b, 0, j))],
        out_specs=[pl.BlockSpec((None, ts, tc), lambda b, j, s: (b, s, j)),
                   pl.BlockSpec((None, CONV_W - 1, tc), lambda b, j, s: (b, 0, j))],
        out_shape=[jax.ShapeDtypeStruct((bsz, seq, c), BF16),
                   jax.ShapeDtypeStruct((bsz, CONV_W - 1, c), F32)],
        scratch_shapes=[pltpu.VMEM((CONV_W - 1, tc), F32)],
        compiler_params=_cparams(("arbitrary", "arbitrary", "arbitrary"), 40),
        name="conv",
    )(z3, z3, z3, conv_w, buf)
    return y.reshape(bsz * seq, c), nb


def _t5_bucket(dist):
    dist = jnp.maximum(dist, 0)
    max_exact = N_BUCKETS // 2
    large = max_exact + (jnp.log(jnp.maximum(dist, 1).astype(F32) / max_exact)
                         / math.log(MAX_DISTANCE / max_exact) * (N_BUCKETS - max_exact)).astype(jnp.int32)
    large = jnp.minimum(large, N_BUCKETS - 1)
    return jnp.where(dist < max_exact, dist, large)


def _bias_kernel(rb_ref, bk_ref, o_ref):
    h = pl.program_id(0)
    bk = bk_ref[...]
    acc = jnp.full(bk.shape, NEG, F32)
    for b in range(N_BUCKETS):
        acc = jnp.where(bk == b, rb_ref[b, h], acc)
    o_ref[...] = acc


def _bias_tiles(rel_bias, buckets):
    nk, r, c = buckets.shape
    return pl.pallas_call(
        _bias_kernel,
        grid=(H_DA, nk),
        in_specs=[pl.BlockSpec(memory_space=pltpu.SMEM),
                  pl.BlockSpec((None, r, c), lambda h, k: (k, 0, 0))],
        out_specs=pl.BlockSpec((None, None, r, c), lambda h, k: (h, k, 0, 0)),
        out_shape=jax.ShapeDtypeStruct((H_DA, nk, r, c), F32),
        compiler_params=_cparams(("arbitrary", "arbitrary"), 16),
        name="bias_tiles",
    )(rel_bias, buckets)


def _attn_tile_kernel(rb_ref, q_ref, k_ref, vt_ref, bias_ref, lam_ref, sub_ref, o_ref,
                      kbf, m1_ref, l1_ref, m2_ref, l2_ref, acc_ref, *, lam_init, t):
    h = pl.program_id(1)
    i = pl.program_id(2)

    @pl.when(i == 0)
    def _cast_keys():
        kbf[...] = k_ref[...].astype(BF16)

    q = q_ref[...] * (DK_DA ** -0.5)
    lane = lax.broadcasted_iota(jnp.int32, q.shape, 1)
    q1 = jnp.where(lane < DK_DA, q, 0.0).astype(BF16)
    q2 = jnp.where(lane >= DK_DA, q, 0.0).astype(BF16)
    lv = lam_ref[...]
    lam = (jnp.exp(jnp.sum(lv[0:1, :] * lv[1:2, :], axis=1, keepdims=True))
           - jnp.exp(jnp.sum(lv[2:3, :] * lv[3:4, :], axis=1, keepdims=True)) + lam_init)
    bfar = rb_ref[N_BUCKETS - 1, h]
    nt = (((1,), (1,)), ((), ()))
    streams = ((q1, m1_ref, l1_ref), (q2, m2_ref, l2_ref))

    def keys(j):
        return kbf[pl.ds(pl.multiple_of(j * t, t), t), :]

    def score(kb, qb):
        return lax.dot_general(kb, qb, nt, preferred_element_type=F32)

    def stats(j, tile):
        kb = keys(j)
        for qb, m_ref, l_ref in streams:
            s = score(kb, qb)
            m_old = m_ref[...]
            if tile is None:
                m_new = jnp.maximum(m_old, jnp.max(s, axis=0, keepdims=True) + bfar)
                e = jnp.exp(s - (m_new - bfar))
            else:
                s = s + tile
                m_new = jnp.maximum(m_old, jnp.max(s, axis=0, keepdims=True))
                e = jnp.exp(s - m_new)
            l_ref[...] = jnp.exp(m_old - m_new) * l_ref[...] + jnp.sum(e, axis=0, keepdims=True)
            m_ref[...] = m_new

    def apply(j, tile):
        kb = keys(j)
        ps = []
        for qb, m_ref, l_ref in streams:
            s = score(kb, qb)
            if tile is None:
                ps.append(jnp.exp(s - (m_ref[...] - bfar)) * l_ref[...])
            else:
                ps.append(jnp.exp(s + tile - m_ref[...]) * l_ref[...])
        a = ps[0] - lam * ps[1]
        acc_ref[...] += jnp.dot(vt_ref[j], a.astype(BF16), preferred_element_type=F32)

    def sweep(fn):
        lax.fori_loop(0, jnp.maximum(i - 1, 0), lambda j, c: (fn(j, None), c)[1], 0)

        @pl.when(i >= 1)
        def _sub_diagonal():
            fn(i - 1, bias_ref[0])

        fn(i, bias_ref[1])

    m1_ref[...] = jnp.full(m1_ref.shape, NEG, F32)
    m2_ref[...] = jnp.full(m2_ref.shape, NEG, F32)
    l1_ref[...] = jnp.zeros(l1_ref.shape, F32)
    l2_ref[...] = jnp.zeros(l2_ref.shape, F32)
    sweep(stats)
    l1_ref[...] = 1.0 / l1_ref[...]
    l2_ref[...] = 1.0 / l2_ref[...]
    acc_ref[...] = jnp.zeros(acc_ref.shape, F32)
    sweep(apply)
    ot = acc_ref[...]
    ms = jnp.mean(ot * ot, axis=0, keepdims=True)
    on = ot * lax.rsqrt(ms + LN_EPS) * sub_ref[...] * (1.0 - lam_init)
    o_ref[...] = on.T.astype(o_ref.dtype)


def _attn_tiles(q, k, vt, rel_bias, bias_near, lam_l, subln_col, bsz, seq, lam_init, t):
    nq = seq // t
    return pl.pallas_call(
        functools.partial(_attn_tile_kernel, lam_init=lam_init, t=t),
        grid=(bsz, H_DA, nq),
        in_specs=[pl.BlockSpec(memory_space=pltpu.SMEM),
                  pl.BlockSpec((None, t, DV_DA), lambda b, h, i: (b, i, h)),
                  pl.BlockSpec((None, seq, DV_DA), lambda b, h, i: (b, 0, h)),
                  pl.BlockSpec((None, None, nq, DV_DA, t), lambda b, h, i: (b, h, 0, 0, 0)),
                  pl.BlockSpec((None, 2, t, t), lambda b, h, i: (h, 0, 0, 0)),
                  pl.BlockSpec((4, DK_DA), lambda b, h, i: (0, 0)),
                  pl.BlockSpec((DV_DA, 1), lambda b, h, i: (0, 0))],
        out_specs=pl.BlockSpec((None, t, DV_DA), lambda b, h, i: (b, i, h)),
        out_shape=jax.ShapeDtypeStruct((bsz, seq, D_BRANCH), BF16),
        scratch_shapes=[pltpu.VMEM((seq, DV_DA), BF16),
                        pltpu.VMEM((1, t), F32), pltpu.VMEM((1, t), F32),
                        pltpu.VMEM((1, t), F32), pltpu.VMEM((1, t), F32),
                        pltpu.VMEM((DV_DA, t), F32)],
        compiler_params=_cparams(("arbitrary", "arbitrary", "arbitrary"), 48),
        name="attn",
    )(rel_bias, q, k, vt, bias_near, lam_l, subln_col)


def _mlstm_kernel(q_ref, k_ref, v_ref, o_ref, i_ref, f_ref, ct0_ref, n0_ref, m0_ref, nrm_ref,
                  h_ref, ct_ref, n_ref, m_ref, ct_s, n_s, m_s):
    c_idx = pl.program_id(1)

    @pl.when(c_idx == 0)
    def _init():
        ct_s[...] = ct0_ref[...]
        n_s[...] = n0_ref[...]
        m_s[...] = m0_ref[...]

    L = q_ref.shape[0]
    r_io = lax.broadcasted_iota(jnp.int32, (L, L), 0)
    c_io = lax.broadcasted_iota(jnp.int32, (L, L), 1)
    upper = (r_io <= c_io).astype(F32)
    nt = (((1,), (1,)), ((), ()))
    for hh in range(H_M):
        qs = slice(hh * DQK_M, (hh + 1) * DQK_M)
        vs = slice(hh * DV_M, (hh + 1) * DV_M)
        q = q_ref[:, qs]
        k = k_ref[:, qs] * (DQK_M ** -0.5)
        v = v_ref[:, vs]
        i_row = i_ref[hh]
        logf = _log_sigmoid(f_ref[hh])
        b_row = jnp.dot(jnp.broadcast_to(logf, (8, L)), upper, precision=HI,
                        preferred_element_type=F32)[0:1, :]
        stack = jnp.where(r_io == 0, b_row, jnp.where(r_io == 1, i_row, 0.0))
        stack_t = stack.T
        b_col = stack_t[:, 0:1]
        i_col = stack_t[:, 1:2]
        d = jnp.where(r_io >= c_io, b_col - b_row + i_row, NEG)
        m_prev = m_s[hh][:, 0:1]
        inter = b_col + m_prev
        m_t = jnp.maximum(inter, jnp.max(d, axis=1, keepdims=True))
        w_inter = jnp.exp(inter - m_t)
        qb = q.astype(BF16)
        kb = k.astype(BF16)
        qk = lax.dot_general(qb, kb, nt, preferred_element_type=F32) * jnp.exp(d - m_t)
        ct = ct_s[hh]
        num = (jnp.dot(qk.astype(BF16), v.astype(BF16), preferred_element_type=F32)
               + w_inter * jnp.dot(qb, ct.astype(BF16), preferred_element_type=F32))
        n_row = n_s[hh]
        den = (jnp.sum(qk, axis=1, keepdims=True)
               + w_inter * jnp.sum(_rnd(q) * _rnd_bits(n_row), axis=1, keepdims=True))
        h = num / jnp.maximum(jnp.abs(den), jnp.exp(-m_t))
        m_new = m_t[L - 1:L, :]
        b_last = b_row[:, L - 1:L]
        w_end = jnp.exp(b_last - b_col + i_col - m_new)
        decay = jnp.exp(b_last + m_prev - m_new)
        ct_new = decay * ct + jnp.dot((w_end * k).T.astype(BF16), v.astype(BF16), preferred_element_type=F32)
        n_new = decay * n_row + jnp.sum(_rnd_bits(w_end) * _rnd(k), axis=0, keepdims=True)
        m_b = jnp.broadcast_to(m_new, (1, LANES))
        ct_s[hh] = ct_new
        n_s[hh] = n_new
        m_s[hh] = m_b
        ct_ref[hh] = ct_new
        n_ref[hh] = n_new
        m_ref[hh] = m_b
        hn = h * lax.rsqrt(jnp.mean(h * h, axis=1, keepdims=True) + LN_EPS) * nrm_ref[...]
        h_ref[:, vs] = (hn * _sigmoid(o_ref[:, vs])).astype(h_ref.dtype)


def _mlstm_prompt(mqk, mv, mo, i_rows, f_rows, ct0, n0, m0, mnorm, bsz, seq):
    L = M_CHUNK if seq % M_CHUNK == 0 else seq
    nc = seq // L
    hq = H_M * DQK_M
    st = lambda b, c: (b, 0, 0, 0)
    gate = pl.BlockSpec((None, H_M, 1, L), lambda b, c: (b, 0, 0, c))
    return pl.pallas_call(
        _mlstm_kernel,
        grid=(bsz, nc),
        in_specs=[pl.BlockSpec((None, L, hq), lambda b, c: (b, c, 0)),
                  pl.BlockSpec((None, L, hq), lambda b, c: (b, c, 1)),
                  pl.BlockSpec((None, L, D_BRANCH), lambda b, c: (b, c, 0)),
                  pl.BlockSpec((None, L, D_BRANCH), lambda b, c: (b, c, 0)),
                  gate, gate,
                  pl.BlockSpec((None, H_M, DQK_M, DV_M), st),
                  pl.BlockSpec((None, H_M, 1, DQK_M), st),
                  pl.BlockSpec((None, H_M, 1, LANES), st),
                  pl.BlockSpec((1, DV_M), lambda b, c: (0, 0))],
        out_specs=[pl.BlockSpec((None, L, D_BRANCH), lambda b, c: (b, c, 0)),
                   pl.BlockSpec((None, H_M, DQK_M, DV_M), st),
                   pl.BlockSpec((None, H_M, 1, DQK_M), st),
                   pl.BlockSpec((None, H_M, 1, LANES), st)],
        out_shape=[jax.ShapeDtypeStruct((bsz, seq, D_BRANCH), BF16),
                   jax.ShapeDtypeStruct((bsz, H_M, DQK_M, DV_M), F32),
                   jax.ShapeDtypeStruct((bsz, H_M, 1, DQK_M), F32),
                   jax.ShapeDtypeStruct((bsz, H_M, 1, LANES), F32)],
        scratch_shapes=[pltpu.VMEM((H_M, DQK_M, DV_M), F32), pltpu.VMEM((H_M, 1, DQK_M), F32),
                        pltpu.VMEM((H_M, 1, LANES), F32)],
        compiler_params=_cparams(("arbitrary", "arbitrary"), 32),
        name="mlstm",
    )(mqk, mqk, mv, mo, i_rows, f_rows, ct0, n0, m0, mnorm)


def _merge_kernel(b0_ref, b1_ref, b2_ref, g0_ref, g1_ref, g2_ref, w_ref, o_ref, wbf_ref):
    @pl.when(pl.program_id(1) == 0)
    def _cast():
        wbf_ref[...] = w_ref[...].astype(BF16)

    acc = _sigmoid(g0_ref[...]) * jnp.dot(b0_ref[...], wbf_ref[0], preferred_element_type=F32)
    acc = acc + _sigmoid(g1_ref[...]) * jnp.dot(b1_ref[...], wbf_ref[1], preferred_element_type=F32)
    acc = acc + _sigmoid(g2_ref[...]) * jnp.dot(b2_ref[...], wbf_ref[2], preferred_element_type=F32)
    o_ref[...] = acc.astype(o_ref.dtype)


def _merge_prompt(br0, br1, br2, zg, w_branch, layer, tm=512, tn=512):
    m, kb = br0.shape
    d = zg.shape[1] // N_BRANCH
    tm = _tile(m, tm)
    tn = _tile(d, tn)
    nn = d // tn
    bspec = pl.BlockSpec((tm, kb), lambda j, i: (i, 0))
    gspec = lambda n: pl.BlockSpec((tm, tn), lambda j, i: (i, n * nn + j))
    return pl.pallas_call(
        _merge_kernel,
        grid=(nn, m // tm),
        in_specs=[bspec, bspec, bspec, gspec(0), gspec(1), gspec(2),
                  pl.BlockSpec((None, N_BRANCH, kb, tn), lambda j, i: (layer, 0, 0, j))],
        out_specs=pl.BlockSpec((tm, tn), lambda j, i: (i, j)),
        out_shape=jax.ShapeDtypeStruct((m, d), BF16),
        scratch_shapes=[pltpu.VMEM((N_BRANCH, kb, tn), BF16)],
        compiler_params=_cparams(("arbitrary", "arbitrary"), 48),
        name="merge",
    )(br0, br1, br2, zg, zg, zg, w_branch)


def _outproj_ln_kernel(a_ref, w_ref, x_ref, g_ref, b_ref, o_ref, obf_ref, acc_ref, *, alpha, tn):
    j = pl.program_id(1)
    nn = acc_ref.shape[0]
    acc_ref[j] = jnp.dot(a_ref[...], w_ref[...].astype(BF16), preferred_element_type=F32)

    @pl.when(j == nn - 1)
    def _ln():
        d = nn * tn
        ys = [alpha * x_ref[:, c * tn:(c + 1) * tn] + acc_ref[c] for c in range(nn)]
        mu = sum(jnp.sum(y, axis=-1, keepdims=True) for y in ys) / d
        ys = [y - mu for y in ys]
        var = sum(jnp.sum(y * y, axis=-1, keepdims=True) for y in ys) / d
        inv = lax.rsqrt(var + LN_EPS)
        for c in range(nn):
            sl = slice(c * tn, (c + 1) * tn)
            out = ys[c] * inv * g_ref[:, sl] + b_ref[:, sl]
            o_ref[:, sl] = out
            obf_ref[:, sl] = out.astype(BF16)


def _outproj_ln(a_bf, w, layer, x, g, b, alpha, tm=512, tn=512):
    m, k = a_bf.shape
    d = x.shape[1]
    tm = _tile(m, tm)
    tn = _tile(d, tn)
    row = lambda i, j: (i, 0)
    return pl.pallas_call(
        functools.partial(_outproj_ln_kernel, alpha=alpha, tn=tn),
        grid=(m // tm, d // tn),
        in_specs=[pl.BlockSpec((tm, k), row),
                  pl.BlockSpec((None, k, tn), lambda i, j: (layer, 0, j)),
                  pl.BlockSpec((tm, d), row),
                  pl.BlockSpec((1, d), lambda i, j: (0, 0)),
                  pl.BlockSpec((1, d), lambda i, j: (0, 0))],
        out_specs=[pl.BlockSpec((tm, d), row), pl.BlockSpec((tm, d), row)],
        out_shape=[jax.ShapeDtypeStruct((m, d), F32), jax.ShapeDtypeStruct((m, d), BF16)],
        scratch_shapes=[pltpu.VMEM((d // tn, tm, tn), F32)],
        compiler_params=_cparams(("arbitrary", "arbitrary"), 48),
        name="outproj_ln",
    )(a_bf, w, x, g, b)


def _route(z):
    lane = lax.broadcasted_iota(jnp.int32, z.shape, 1)
    lane_f = lane.astype(F32)
    big = float(LANES)
    gmask = lane < N_GROUPS
    lg = jnp.where(gmask, z, NEG)
    gmax = jnp.max(lg, axis=1, keepdims=True)
    gsel = jnp.min(jnp.where(gmask & (lg == gmax), lane_f, big), axis=1, keepdims=True)
    pg_sel = 1.0 / jnp.sum(jnp.where(gmask, jnp.exp(lg - gmax), 0.0), axis=1, keepdims=True)
    lo = N_GROUPS + EXP_PER_GROUP * gsel
    emask = (lane_f >= lo) & (lane_f < lo + EXP_PER_GROUP)
    le = jnp.where(emask, z, NEG)
    emax = jnp.max(le, axis=1, keepdims=True)
    pe = jnp.where(emask, jnp.exp(le - emax), 0.0)
    pe = pe / jnp.sum(pe, axis=1, keepdims=True)
    pm = jnp.where(emask, pe, -1.0)
    p1 = jnp.max(pm, axis=1, keepdims=True)
    i1 = jnp.min(jnp.where(pm == p1, lane_f, big), axis=1, keepdims=True)
    pm2 = jnp.where(lane_f == i1, -1.0, pm)
    p2 = jnp.max(pm2, axis=1, keepdims=True)
    i2 = jnp.min(jnp.where(pm2 == p2, lane_f, big), axis=1, keepdims=True)
    tot = p1 + p2
    w1 = pg_sel * p1 / tot
    w2 = pg_sel * p2 / tot
    wts = jnp.where(lane == 0, w1, jnp.where(lane == 1, w2, 0.0))
    ids = jnp.where(lane == 0, i1 - N_GROUPS, jnp.where(lane == 1, i2 - N_GROUPS, 0.0)).astype(jnp.int32)
    return wts, ids


def _router_kernel(x_ref, w_ref, b_ref, wt_ref, id_ref):
    z = jnp.dot(x_ref[...], w_ref[...].astype(BF16), preferred_element_type=F32) + b_ref[...]
    wts, ids = _route(z)
    wt_ref[...] = wts
    id_ref[...] = ids


def _router(x, w_r, b_r, tm=256):
    m, k = x.shape
    tm = _tile(m, tm)
    return pl.pallas_call(
        _router_kernel,
        grid=(m // tm,),
        in_specs=[pl.BlockSpec((tm, k), lambda i: (i, 0)),
                  pl.BlockSpec((k, LANES), lambda i: (0, 0)),
                  pl.BlockSpec((1, LANES), lambda i: (0, 0))],
        out_specs=[pl.BlockSpec((tm, LANES), lambda i: (i, 0)), pl.BlockSpec((tm, LANES), lambda i: (i, 0))],
        out_shape=[jax.ShapeDtypeStruct((m, LANES), F32), jax.ShapeDtypeStruct((m, LANES), jnp.int32)],
        compiler_params=_cparams(("arbitrary",), 32),
        name="router",
    )(x, w_r, b_r)


def _router_weights(w_rg, b_rg, w_re, b_re):
    k = w_rg.shape[0]
    w = jnp.zeros((k, LANES), F32).at[:, :N_GROUPS].set(w_rg).at[:, N_GROUPS:N_GROUPS + N_EXPERTS].set(w_re)
    b = jnp.zeros((1, LANES), F32).at[0, :N_GROUPS].set(b_rg).at[0, N_GROUPS:N_GROUPS + N_EXPERTS].set(b_re)
    return w, b


def _row_copy(src_hbm, row, dst_ref, r, sem):
    return pltpu.make_async_copy(src_hbm.at[pl.ds(row, 1)], dst_ref.at[pl.ds(r, 1)], sem)


def _gather_kernel(src_ref, x_hbm, o_ref, sem):
    i = pl.program_id(0)
    te = o_ref.shape[0]

    def issue(r2, c):
        r = 2 * r2
        _row_copy(x_hbm, src_ref[i * te + r], o_ref, r, sem).start(priority=0)
        _row_copy(x_hbm, src_ref[i * te + r + 1], o_ref, r + 1, sem).start(priority=1)
        return c

    lax.fori_loop(0, te // 2, issue, 0)

    def wait(r, c):
        _row_copy(x_hbm, 0, o_ref, r, sem).wait()
        return c

    lax.fori_loop(0, te, wait, 0)


def _gather_rows(x, row_src, te):
    nr = row_src.shape[0]
    d = x.shape[1]
    grid_spec = pltpu.PrefetchScalarGridSpec(
        num_scalar_prefetch=1,
        grid=(nr // te,),
        in_specs=[pl.BlockSpec(memory_space=pl.ANY)],
        out_specs=pl.BlockSpec((te, d), lambda i, s: (i, 0)),
        scratch_shapes=[pltpu.SemaphoreType.DMA(())])
    return pl.pallas_call(
        _gather_kernel,
        grid_spec=grid_spec,
        out_shape=jax.ShapeDtypeStruct((nr, d), x.dtype),
        compiler_params=_cparams(("arbitrary",), 32),
        name="moe_gather",
    )(row_src, x)


def _expert_kernel(te_ref, first_ref, nused_ref, xs_ref, rw_ref, wg_ref, wu_ref, wd_ref, y_ref,
                   wg_bf, wu_bf, wd_bf):
    i = pl.program_id(0)

    @pl.when(first_ref[i] == 1)
    def _cast():
        wg_bf[...] = wg_ref[...].astype(BF16)
        wu_bf[...] = wu_ref[...].astype(BF16)
        wd_bf[...] = wd_ref[...].astype(BF16)

    @pl.when(i < nused_ref[0])
    def _compute():
        xb = xs_ref[...].astype(BF16)
        g = jnp.dot(xb, wg_bf[...], preferred_element_type=F32)
        u = jnp.dot(xb, wu_bf[...], preferred_element_type=F32)
        hid = g * _sigmoid(g) * u * rw_ref[...]
        y_ref[...] = jnp.dot(hid.astype(BF16), wd_bf[...], preferred_element_type=F32)

    @pl.when(i >= nused_ref[0])
    def _pad():
        y_ref[...] = jnp.zeros(y_ref.shape, y_ref.dtype)


def _expert_mlp(xs, row_w, tile_expert, tile_first, n_used, w_gate, w_up, w_down, layer, te):
    nr, d = xs.shape
    f = w_gate.shape[-1]
    grid_spec = pltpu.PrefetchScalarGridSpec(
        num_scalar_prefetch=3,
        grid=(nr // te,),
        in_specs=[pl.BlockSpec((te, d), lambda i, e, fi, nu: (i, 0)),
                  pl.BlockSpec((te, 1), lambda i, e, fi, nu: (i, 0)),
                  pl.BlockSpec((None, None, d, f), lambda i, e, fi, nu: (layer, e[i], 0, 0)),
                  pl.BlockSpec((None, None, d, f), lambda i, e, fi, nu: (layer, e[i], 0, 0)),
                  pl.BlockSpec((None, None, f, d), lambda i, e, fi, nu: (layer, e[i], 0, 0))],
        out_specs=pl.BlockSpec((te, d), lambda i, e, fi, nu: (i, 0)),
        scratch_shapes=[pltpu.VMEM((d, f), BF16), pltpu.VMEM((d, f), BF16), pltpu.VMEM((f, d), BF16)])
    return pl.pallas_call(
        _expert_kernel,
        grid_spec=grid_spec,
        out_shape=jax.ShapeDtypeStruct((nr, d), F32),
        compiler_params=_cparams(("arbitrary",), 52),
        name="moe_experts",
    )(tile_expert, tile_first, n_used, xs, row_w, w_gate, w_up, w_down)


def _combine_ln_kernel(pos_ref, x_ref, ys_hbm, g_ref, b_ref, o_ref, obf_ref, buf0, buf1, sem, *, alpha):
    i = pl.program_id(0)
    tm = x_ref.shape[0]

    def issue(r, c):
        t = i * tm + r
        _row_copy(ys_hbm, pos_ref[2 * t], buf0, r, sem.at[0]).start(priority=0)
        _row_copy(ys_hbm, pos_ref[2 * t + 1], buf1, r, sem.at[1]).start(priority=1)
        return c

    lax.fori_loop(0, tm, issue, 0)

    def wait(r, c):
        _row_copy(ys_hbm, 0, buf0, r, sem.at[0]).wait()
        _row_copy(ys_hbm, 0, buf1, r, sem.at[1]).wait()
        return c

    lax.fori_loop(0, tm, wait, 0)
    y = alpha * x_ref[...] + (buf0[...] + buf1[...])
    out = _layer_norm_rows(y, g_ref[...], b_ref[...])
    o_ref[...] = out
    obf_ref[...] = out.astype(BF16)


def _combine_ln(x, ys, pos, g, b, alpha, tm=256):
    m, d = x.shape
    tm = _tile(m, tm)
    grid_spec = pltpu.PrefetchScalarGridSpec(
        num_scalar_prefetch=1,
        grid=(m // tm,),
        in_specs=[pl.BlockSpec((tm, d), lambda i, p: (i, 0)),
                  pl.BlockSpec(memory_space=pl.ANY),
                  pl.BlockSpec((1, d), lambda i, p: (0, 0)),
                  pl.BlockSpec((1, d), lambda i, p: (0, 0))],
        out_specs=[pl.BlockSpec((tm, d), lambda i, p: (i, 0)), pl.BlockSpec((tm, d), lambda i, p: (i, 0))],
        scratch_shapes=[pltpu.VMEM((tm, d), F32), pltpu.VMEM((tm, d), F32), pltpu.SemaphoreType.DMA((2,))])
    return pl.pallas_call(
        functools.partial(_combine_ln_kernel, alpha=alpha),
        grid_spec=grid_spec,
        out_shape=[jax.ShapeDtypeStruct((m, d), F32), jax.ShapeDtypeStruct((m, d), BF16)],
        compiler_params=_cparams(("arbitrary",), 40),
        name="moe_combine_ln",
    )(pos, x, ys, g, b)


def _dispatch_plan(ids, wts, te):
    t = ids.shape[0]
    a = 2 * t
    nr = a + N_EXPERTS * te
    e_flat = ids.reshape(a)
    w_flat = wts.reshape(a)
    order = jnp.argsort(e_flat, stable=True).astype(jnp.int32)
    onehot = jax.nn.one_hot(e_flat, N_EXPERTS, dtype=jnp.int32)
    counts = jnp.sum(onehot, axis=0)
    padded = ((counts + te - 1) // te) * te
    pad_end = jnp.cumsum(padded)
    pad_off = pad_end - padded
    off = jnp.cumsum(counts) - counts
    rank = jnp.sum((jnp.cumsum(onehot, axis=0) - onehot) * onehot, axis=1)
    pos = (pad_off[e_flat] + rank).astype(jnp.int32)
    n_used = (pad_end[-1] // te).astype(jnp.int32)
    tile_start = jnp.arange(nr // te, dtype=jnp.int32) * te
    tile_expert = jnp.sum((pad_end[None, :] <= tile_start[:, None]).astype(jnp.int32), axis=1)
    rows = jnp.arange(nr, dtype=jnp.int32)
    e_row = jnp.minimum(jnp.repeat(tile_expert, te), N_EXPERTS - 1)
    local = rows - pad_off[e_row]
    valid = (rows < pad_end[-1]) & (local < counts[e_row])
    src = order[jnp.clip(off[e_row] + local, 0, a - 1)]
    row_src = jnp.where(valid, src // 2, 0).astype(jnp.int32)
    row_w = jnp.where(valid, w_flat[src], 0.0).reshape(nr, 1)
    last_e = jnp.max(e_flat).astype(jnp.int32)
    tile_expert = jnp.minimum(tile_expert, last_e).astype(jnp.int32)
    prev = jnp.concatenate([jnp.full((1,), -1, jnp.int32), tile_expert[:-1]])
    tile_first = (tile_expert != prev).astype(jnp.int32)
    return row_src, row_w, pos, tile_expert, tile_first, n_used.reshape(1)


def _small_mm_kernel(x_ref, w_ref, b_ref, o_ref):
    o_ref[...] = jnp.dot(x_ref[...].astype(BF16), w_ref[...].astype(BF16), preferred_element_type=F32) + b_ref[...]


def _small_mm_t_kernel(x_ref, w_ref, b_ref, o_ref):
    o_ref[...] = lax.dot_general(x_ref[...].astype(BF16), w_ref[...].astype(BF16), (((1,), (1,)), ((), ())),
                                 preferred_element_type=F32) + b_ref[...]


def _small_mm(x, w, lead, bias=None, tn=1024, transposed=False):
    r, k = x.shape
    n = w.shape[-2] if transposed else w.shape[-1]
    tn = min(tn, n)
    nlead = len(lead)
    if bias is None:
        bias = jnp.zeros((1, n), F32)
    if transposed:
        w_spec = pl.BlockSpec((None,) * nlead + (tn, k), lambda j: tuple(lead) + (j, 0))
    else:
        w_spec = pl.BlockSpec((None,) * nlead + (k, tn), lambda j: tuple(lead) + (0, j))
    return pl.pallas_call(
        _small_mm_t_kernel if transposed else _small_mm_kernel,
        grid=(pl.cdiv(n, tn),),
        in_specs=[pl.BlockSpec((r, k), lambda j: (0, 0)),
                  w_spec,
                  pl.BlockSpec((1, tn), lambda j: (0, j))],
        out_specs=pl.BlockSpec((r, tn), lambda j: (0, j)),
        out_shape=jax.ShapeDtypeStruct((r, n), F32),
        compiler_params=_cparams(("arbitrary",), 40),
        name="small_mm",
    )(x, w, bias)


def _small_call(fn, out_shapes, *args, name="small"):
    n_in = len(args)

    def kern(*refs):
        outs = fn(*[r[...] for r in refs[:n_in]])
        for r, o in zip(refs[n_in:], outs):
            r[...] = o

    return pl.pallas_call(kern, out_shape=out_shapes, name=name)(*args)


def _sample_conv_fn(zb, zc, zx, buf0, buf1, w):
    u = zc * zx
    y = buf0 * w[0:1, :]
    y = y + buf1 * w[1:2, :]
    y = y + u * w[2:3, :]
    return zb * y, u


def _sample_mlstm_kernel(q_ref, k_ref, v_ref, vc_ref, o_ref, i_ref, f_ref, c_ref, n_ref, m_ref, nrm_ref,
                         h_ref, c_out, n_out, m_out):
    for hh in range(H_M):
        q = q_ref[hh]
        k = k_ref[hh] * (DQK_M ** -0.5)
        v = v_ref[hh]
        v_col = vc_ref[hh]
        ig = i_ref[hh][:, 0:1]
        logf = _log_sigmoid(f_ref[hh][:, 0:1])
        m_prev = m_ref[hh][:, 0:1]
        c = c_ref[hh]
        n = n_ref[hh]
        inter = logf + m_prev
        m_t = jnp.maximum(inter, ig)
        w_inter = jnp.exp(inter - m_t)
        qk = jnp.sum(q * k, axis=1, keepdims=True) * jnp.exp(ig - m_t)
        cq = lax.dot_general(jnp.broadcast_to(q, (8, DQK_M)).astype(BF16), c.astype(BF16),
                             (((1,), (1,)), ((), ())), preferred_element_type=F32)[0:1, :]
        num = qk * v + w_inter * cq
        den = qk + w_inter * jnp.sum(n * q, axis=1, keepdims=True)
        h = num / jnp.maximum(jnp.abs(den), jnp.exp(-m_t))
        w_end = jnp.exp(ig - m_t)
        decay = jnp.exp(logf + m_prev - m_t)
        c_out[hh] = decay * c + w_end * (v_col * k)
        n_out[hh] = decay * n + w_end * k
        m_out[hh] = jnp.broadcast_to(m_t, (1, LANES))
        hn = h * lax.rsqrt(jnp.mean(h * h, axis=1, keepdims=True) + LN_EPS) * nrm_ref[...]
        h_ref[hh] = hn * _sigmoid(o_ref[hh])


def _sample_mlstm(mq, mk, mv, mo, mi, mf, c0, n0, m0, mnorm):
    r = mq.shape[0]
    mv_col = mv.reshape(r, H_M, DV_M, 1)
    blk = lambda *s: pl.BlockSpec((None,) + s, lambda b: (b,) + (0,) * len(s))
    return pl.pallas_call(
        _sample_mlstm_kernel,
        grid=(r,),
        in_specs=[blk(H_M, 1, DQK_M), blk(H_M, 1, DQK_M), blk(H_M, 1, DV_M), blk(H_M, DV_M, 1), blk(H_M, 1, DV_M),
                  blk(H_M, 1, LANES), blk(H_M, 1, LANES), blk(H_M, DV_M, DQK_M), blk(H_M, 1, DQK_M),
                  blk(H_M, 1, LANES), pl.BlockSpec((1, DV_M), lambda b: (0, 0))],
        out_specs=[blk(H_M, 1, DV_M), blk(H_M, DV_M, DQK_M), blk(H_M, 1, DQK_M), blk(H_M, 1, LANES)],
        out_shape=[jax.ShapeDtypeStruct((r, H_M, 1, DV_M), F32),
                   jax.ShapeDtypeStruct((r, H_M, DV_M, DQK_M), F32),
                   jax.ShapeDtypeStruct((r, H_M, 1, DQK_M), F32),
                   jax.ShapeDtypeStruct((r, H_M, 1, LANES), F32)],
        compiler_params=_cparams(("arbitrary",), 32),
        name="sample_mlstm",
    )(mq, mk, mv, mv_col, mo, mi, mf, c0, n0, m0, mnorm)


def _class_reduce(x, op):
    sh = H_DA
    while sh < x.shape[-1]:
        x = op(x, pltpu.roll(x, sh, x.ndim - 1))
        sh *= 2
    return x


def _paged_attn_kernel(pt_ref, q_ref, kn_ref, vn_ref, *rest, lam_init, group):
    k_refs, v_refs = rest[:group], rest[group:2 * group]
    (brow_ref, bself_ref, lam_ref, sub_ref, o_ref, qsel, sc, m_run, l_run, mb, ilb, acc, lam_s) = rest[2 * group:]
    sweep = pl.program_id(1)
    p = pl.program_id(2)
    last_step = pl.num_programs(2) - 1
    rows = k_refs[0].shape[0]
    nt = (((1,), (1,)), ((), ()))
    r_io = lax.broadcasted_iota(jnp.int32, (2 * H_DA, rows), 0)
    c_io = lax.broadcasted_iota(jnp.int32, (2 * H_DA, rows), 1)
    own = (c_io % H_DA) == (r_io % H_DA)
    lane = lax.broadcasted_iota(jnp.int32, (H_DA, DV_DA), 1)
    first = lane < DK_DA

    def to_lanes(col):
        return jnp.sum(jnp.where(own[0:H_DA], col, 0.0), axis=0, keepdims=True)

    def to_col(row):
        return jnp.max(jnp.where(own[0:H_DA], row, NEG), axis=1, keepdims=True)

    def self_scores():
        prod = _rnd_bits(kn_ref[...]) * _rnd_bits(q_ref[...] * (DK_DA ** -0.5))
        t1 = jnp.sum(jnp.where(first, prod, 0.0), axis=1, keepdims=True) + bself_ref[...]
        t2 = jnp.sum(jnp.where(first, 0.0, prod), axis=1, keepdims=True) + bself_ref[...]
        return t1, t2

    @pl.when((sweep == 0) & (p == 0))
    def _init():
        q = q_ref[...] * (DK_DA ** -0.5)
        qsel[...] = jnp.concatenate([jnp.where(first, q, 0.0), jnp.where(first, 0.0, q)], axis=0).astype(BF16)
        m_run[...] = jnp.full(m_run.shape, NEG, F32)
        l_run[...] = jnp.zeros(l_run.shape, F32)
        lv = lam_ref[...]
        lam = (jnp.exp(jnp.sum(lv[0:1, :] * lv[1:2, :], axis=1, keepdims=True))
               - jnp.exp(jnp.sum(lv[2:3, :] * lv[3:4, :], axis=1, keepdims=True)) + lam_init)
        lam_s[...] = jnp.broadcast_to(lam, lam_s.shape)

    @pl.when(sweep == 0)
    def _scores():
        for i, k_ref in enumerate(k_refs):
            r = lax.dot_general(qsel[...], k_ref[...].astype(BF16), nt, preferred_element_type=F32)
            r = jnp.where(own, r, 0.0)
            if i == group - 1:
                bias = jnp.where(p == last_step, brow_ref[1:2, :], brow_ref[0:1, :])
            else:
                bias = brow_ref[0:1, :]
            for c in range(2):
                s = jnp.sum(r[c * H_DA:(c + 1) * H_DA], axis=0, keepdims=True) + bias
                sc[p * group + i, c:c + 1, :] = s
                m_old = m_run[c:c + 1, :]
                m_new = jnp.maximum(m_old, s)
                l_run[c:c + 1, :] = l_run[c:c + 1, :] * jnp.exp(m_old - m_new) + jnp.exp(s - m_new)
                m_run[c:c + 1, :] = m_new

        @pl.when(p == last_step)
        def _close():
            for c, t in enumerate(self_scores()):
                tb = to_lanes(t)
                m_c = m_run[c:c + 1, :]
                m_all = jnp.maximum(_class_reduce(m_c, jnp.maximum), tb)
                l_all = _class_reduce(l_run[c:c + 1, :] * jnp.exp(m_c - m_all), jnp.add) + jnp.exp(tb - m_all)
                mb[c:c + 1, :] = m_all
                ilb[c:c + 1, :] = 1.0 / l_all
            acc[...] = jnp.zeros(acc.shape, F32)

    @pl.when(sweep == 1)
    def _apply():
        lam = lam_s[:, 0:1]
        part = acc[...]
        for i, v_ref in enumerate(v_refs):
            pg = p * group + i
            w1 = jnp.exp(sc[pg, 0:1, :] - mb[0:1, :]) * ilb[0:1, :]
            w2 = jnp.exp(sc[pg, 1:2, :] - mb[1:2, :]) * ilb[1:2, :]
            a = jnp.where(own, w1 - lam * w2, 0.0).astype(BF16)
            part = part + jnp.dot(a, v_ref[...].astype(BF16), preferred_element_type=F32)
        acc[...] = part

        @pl.when(p == last_step)
        def _fin():
            t1, t2 = self_scores()
            w1s = jnp.exp(t1 - to_col(mb[0:1, :])) * to_col(ilb[0:1, :])
            w2s = jnp.exp(t2 - to_col(mb[1:2, :])) * to_col(ilb[1:2, :])
            o = acc[0:H_DA, :] + _rnd_bits(w1s - lam * w2s) * _rnd_bits(vn_ref[...])
            on = o * lax.rsqrt(jnp.mean(o * o, axis=1, keepdims=True) + LN_EPS) * sub_ref[...] * (1.0 - lam_init)
            o_ref[...] = on


def _paged_attn(q, k_new, v_new, cache_k, cache_v, page_table, layer, bias_rows, bias_self, lam_l, subln_row,
                lam_init):
    r = q.shape[0]
    npg = page_table.shape[1]
    psz = cache_k.shape[2]
    rows = psz * H_DA
    pt = page_table.reshape(-1)
    tok = lambda b, s, p, t: (b, 0, 0)
    cst2 = lambda b, s, p, t: (0, 0)
    group = _tile(npg, 8)
    nsteps = npg // group

    def kpage(i):
        return lambda b, s, p, t: (layer, t[b * npg + (p * (1 - s) + (nsteps - 1) * s) * group + i], 0, 0)

    def vpage(i):
        return lambda b, s, p, t: (layer, t[b * npg + p * s * group + i], 0, 0)

    pool = cache_k.shape[1]
    cache_k = cache_k.reshape(cache_k.shape[0], pool, rows, DV_DA)
    cache_v = cache_v.reshape(cache_v.shape[0], pool, rows, DV_DA)
    grid_spec = pltpu.PrefetchScalarGridSpec(
        num_scalar_prefetch=1,
        grid=(r, 2, nsteps),
        in_specs=[pl.BlockSpec((None, H_DA, DV_DA), tok),
                  pl.BlockSpec((None, H_DA, DV_DA), tok),
                  pl.BlockSpec((None, H_DA, DV_DA), tok)]
                 + [pl.BlockSpec((None, None, rows, DV_DA), kpage(i)) for i in range(group)]
                 + [pl.BlockSpec((None, None, rows, DV_DA), vpage(i)) for i in range(group)]
                 + [pl.BlockSpec((2, rows), cst2),
                  pl.BlockSpec((H_DA, 1), cst2),
                  pl.BlockSpec((4, DK_DA), cst2),
                  pl.BlockSpec((1, DV_DA), cst2)],
        out_specs=pl.BlockSpec((None, H_DA, DV_DA), tok),
        scratch_shapes=[pltpu.VMEM((2 * H_DA, DV_DA), BF16), pltpu.VMEM((npg, 2, rows), F32),
                        pltpu.VMEM((2, rows), F32), pltpu.VMEM((2, rows), F32),
                        pltpu.VMEM((2, rows), F32), pltpu.VMEM((2, rows), F32),
                        pltpu.VMEM((2 * H_DA, DV_DA), F32), pltpu.VMEM((1, LANES), F32)])
    return pl.pallas_call(
        functools.partial(_paged_attn_kernel, lam_init=lam_init, group=group),
        grid_spec=grid_spec,
        out_shape=jax.ShapeDtypeStruct((r, H_DA, DV_DA), F32),
        compiler_params=_cparams(("arbitrary", "arbitrary", "arbitrary"), 40),
        name="paged_attn",
    )(pt, q, k_new, v_new, *([cache_k] * group), *([cache_v] * group), bias_rows, bias_self, lam_l, subln_row)


def _sample_moe_kernel(x_ref, cmb_ref, wg_ref, wu_ref, wd_ref, y_ref):
    e = pl.program_id(0)

    @pl.when(e == 0)
    def _init():
        y_ref[...] = jnp.zeros(y_ref.shape, F32)

    x = x_ref[...].astype(BF16)
    g = jnp.dot(x, wg_ref[...].astype(BF16), preferred_element_type=F32)
    u = jnp.dot(x, wu_ref[...].astype(BF16), preferred_element_type=F32)
    hid = g * _sigmoid(g) * u * cmb_ref[...]
    y_ref[...] += jnp.dot(hid.astype(BF16), wd_ref[...].astype(BF16), preferred_element_type=F32)


def _sample_moe(x, combine_t, w_gate, w_up, w_down, layer):
    r, d = x.shape
    f = w_gate.shape[-1]
    return pl.pallas_call(
        _sample_moe_kernel,
        grid=(N_EXPERTS,),
        in_specs=[pl.BlockSpec((r, d), lambda e: (0, 0)),
                  pl.BlockSpec((None, r, 1), lambda e: (e, 0, 0)),
                  pl.BlockSpec((None, None, d, f), lambda e: (layer, e, 0, 0)),
                  pl.BlockSpec((None, None, d, f), lambda e: (layer, e, 0, 0)),
                  pl.BlockSpec((None, None, f, d), lambda e: (layer, e, 0, 0))],
        out_specs=pl.BlockSpec((r, d), lambda e: (0, 0)),
        out_shape=jax.ShapeDtypeStruct((r, d), F32),
        compiler_params=_cparams(("arbitrary",), 48),
        name="sample_moe",
    )(x, combine_t, w_gate, w_up, w_down)


def _prompt_layer(l, x, x_bf, bsz, seq, bias_t, attn_t, moe_te, p):
    f32 = F32
    alpha = (2.0 * p["depth"]) ** 0.25
    lam_init = 0.8 - 0.6 * math.exp(-0.3 * l)
    t = bsz * seq
    w_in_t, b_in = p["w_in_t"], p["b_in"]
    bseg = lambda o, n: b_in[l, o:o + n].reshape(1, n)
    pr = lambda o, n, dt=f32: _proj(x_bf, w_in_t, l, o, n, bseg(o, n), dt)
    z_conv = pr(OFF_CONV, 3 * D_BRANCH)
    a_q = pr(OFF_AQ, D_BRANCH)
    a_k = pr(OFF_AK, D_BRANCH)
    a_v = pr(OFF_AV, D_BRANCH)
    m_qk = pr(OFF_MQ, 2 * H_M * DQK_M)
    m_v = pr(OFF_MV, D_BRANCH)
    m_o = pr(OFF_MO, D_BRANCH)
    z_g = pr(OFF_G, N_BRANCH * x.shape[1])
    z_if = pr(OFF_MI, LANES)

    y_conv, conv_buf = _conv_prompt(z_conv, p["conv_w"][l], jnp.zeros((bsz, CONV_W - 1, D_BRANCH), f32), bsz, seq)

    vt = a_v.reshape(bsz, seq // attn_t, attn_t, H_DA, DV_DA).transpose(0, 3, 1, 4, 2).astype(BF16)
    o = _attn_tiles(a_q.reshape(bsz, seq, D_BRANCH), a_k.reshape(bsz, seq, D_BRANCH), vt, p["rel_bias"], bias_t,
                    p["attn_lambda"][l], p["attn_subln"][l].reshape(DV_DA, 1), bsz, seq, lam_init, attn_t)
    o = o.reshape(t, D_BRANCH)

    gates = z_if[:, :2 * H_M].reshape(bsz, seq, 2, H_M).transpose(2, 0, 3, 1).reshape(2, bsz, H_M, 1, seq)
    h, ct1, n1, m1 = _mlstm_prompt(
        m_qk.reshape(bsz, seq, D_BRANCH), m_v.reshape(bsz, seq, D_BRANCH), m_o.reshape(bsz, seq, D_BRANCH),
        gates[0], gates[1],
        jnp.zeros((bsz, H_M, DQK_M, DV_M), f32), jnp.zeros((bsz, H_M, 1, DQK_M), f32),
        jnp.zeros((bsz, H_M, 1, LANES), f32), p["mlstm_norm"][l].reshape(1, DV_M), bsz, seq)
    h = h.reshape(t, D_BRANCH)

    merged = _merge_prompt(y_conv, o, h, z_g, p["w_branch"], l)
    x1, x1_bf = _outproj_ln(merged, p["w_out"], l, x, p["ln_g"][l, 0].reshape(1, -1),
                            p["ln_b"][l, 0].reshape(1, -1), alpha)

    w_r, b_r = p["router"][l]
    wts, ids = _router(x1_bf, w_r, b_r)
    row_src, row_w, pos, tile_e, tile_first, n_used = _dispatch_plan(ids[:, :2], wts[:, :2], moe_te)
    xs = _gather_rows(x1, row_src, moe_te)
    ys = _expert_mlp(xs, row_w, tile_e, tile_first, n_used, p["w_gate"], p["w_up"], p["w_down"], l, moe_te)
    x2, x2_bf = _combine_ln(x1, ys, pos, p["ln_g"][l, 1].reshape(1, -1), p["ln_b"][l, 1].reshape(1, -1), alpha)

    state = (a_k.reshape(bsz, seq, H_DA, 2 * DK_DA), a_v.reshape(bsz, seq, H_DA, DV_DA), conv_buf,
             jnp.swapaxes(ct1, -1, -2), n1.reshape(bsz, H_M, DQK_M), m1[:, :, 0, 0])
    return x2, x2_bf, state


def _sample_layer(l, x, conv_state, c0, n0, m0, cache_k, cache_v, page_table, dec_bias, p):
    f32 = F32
    alpha = (2.0 * p["depth"]) ** 0.25
    lam_init = 0.8 - 0.6 * math.exp(-0.3 * l)
    r, d = x.shape
    z = _small_mm(x, p["w_in_t"], (l,), p["b_in"][l].reshape(1, -1), transposed=True)
    seg = lambda o, n: z[:, o:o + n]
    y_conv, u = _small_call(
        _sample_conv_fn, [jax.ShapeDtypeStruct((r, D_BRANCH), f32)] * 2,
        seg(OFF_CONV, D_BRANCH), seg(OFF_CONV + D_BRANCH, D_BRANCH), seg(OFF_CONV + 2 * D_BRANCH, D_BRANCH),
        conv_state[:, 0], conv_state[:, 1], p["conv_w"][l], name="sample_conv")
    new_buf = jnp.stack([conv_state[:, 1], u], axis=1)

    a_q = seg(OFF_AQ, D_BRANCH).reshape(r, H_DA, DV_DA)
    a_k = seg(OFF_AK, D_BRANCH).reshape(r, H_DA, DV_DA)
    a_v = seg(OFF_AV, D_BRANCH).reshape(r, H_DA, DV_DA)
    bias_rows, bias_self = dec_bias
    o = _paged_attn(a_q, a_k, a_v, cache_k, cache_v, page_table, l, bias_rows, bias_self,
                    p["attn_lambda"][l], p["attn_subln"][l].reshape(1, DV_DA), lam_init).reshape(r, D_BRANCH)

    lanes = lambda a: jnp.broadcast_to(a.reshape(r, H_M, 1, 1), (r, H_M, 1, LANES))
    h, c1, n1, m1 = _sample_mlstm(
        seg(OFF_MQ, H_M * DQK_M).reshape(r, H_M, 1, DQK_M), seg(OFF_MK, H_M * DQK_M).reshape(r, H_M, 1, DQK_M),
        seg(OFF_MV, D_BRANCH).reshape(r, H_M, 1, DV_M), seg(OFF_MO, D_BRANCH).reshape(r, H_M, 1, DV_M),
        lanes(seg(OFF_MI, H_M)), lanes(seg(OFF_MF, H_M)),
        c0, n0.reshape(r, H_M, 1, DQK_M), lanes(m0), p["mlstm_norm"][l].reshape(1, DV_M))
    h = h.reshape(r, D_BRANCH)

    projs = [_small_mm(br, p["w_branch"], (l, n)) for n, br in enumerate((y_conv, o, h))]
    g = seg(OFF_G, N_BRANCH * d)

    def merge_fn(g0, g1, g2, p0, p1, p2):
        return (_sigmoid(g0) * p0 + _sigmoid(g1) * p1 + _sigmoid(g2) * p2,)

    merged, = _small_call(merge_fn, [jax.ShapeDtypeStruct((r, d), f32)],
                          g[:, :d], g[:, d:2 * d], g[:, 2 * d:], *projs, name="sample_merge")
    mix = _small_mm(merged, p["w_out"], (l,))

    def ln_fn(xr, yr, gg, bb):
        return (_layer_norm_rows(alpha * xr + yr, gg, bb),)

    x1, = _small_call(ln_fn, [jax.ShapeDtypeStruct((r, d), f32)], x, mix,
                      p["ln_g"][l, 0].reshape(1, -1), p["ln_b"][l, 0].reshape(1, -1), name="sample_ln")
    w_r, b_r = p["router"][l]
    zr = _small_mm(x1, w_r, (), b_r, tn=LANES)
    wts, ids = _small_call(_route, [jax.ShapeDtypeStruct((r, LANES), f32), jax.ShapeDtypeStruct((r, LANES), jnp.int32)],
                           zr, name="sample_route")
    combine = jnp.einsum("tk,tke->te", wts[:, :2], jax.nn.one_hot(ids[:, :2], N_EXPERTS, dtype=f32))
    y = _sample_moe(x1, combine.T.reshape(N_EXPERTS, r, 1), p["w_gate"], p["w_up"], p["w_down"], l)
    x2, = _small_call(ln_fn, [jax.ShapeDtypeStruct((r, d), f32)], x1, y,
                      p["ln_g"][l, 1].reshape(1, -1), p["ln_b"][l, 1].reshape(1, -1), name="sample_ln")
    state = (a_k.reshape(r, 1, H_DA, 2 * DK_DA), a_v.reshape(r, 1, H_DA, DV_DA), new_buf,
             c1, n1.reshape(r, H_M, DQK_M), m1[:, :, 0, 0])
    return x2, state


def kernel(x_prompt, x_sample, cache_k, cache_v, state_conv, state_mlstm_c, state_mlstm_n, state_mlstm_m,
           page_table, rel_bias, w_in, b_in, conv_w, attn_lambda, attn_subln, mlstm_norm, w_branch, w_out,
           ln_g, ln_b, w_router_group, b_router_group, w_router_expert, b_router_expert, w_gate, w_up, w_down):
    depth = w_in.shape[0]
    bp, sp, d = x_prompt.shape
    bs, ss, _ = x_sample.shape
    assert ss == 1, "the sample group decodes one token per sequence"
    psz = cache_k.shape[2]
    past_len = page_table.shape[1] * psz
    assert psz >= MAX_DISTANCE, "only the last page may hold keys closer than MAX_DISTANCE"

    params = dict(
        depth=depth, rel_bias=rel_bias, b_in=b_in, conv_w=conv_w, attn_lambda=attn_lambda, attn_subln=attn_subln,
        mlstm_norm=mlstm_norm, w_branch=w_branch, w_out=w_out, ln_g=ln_g, ln_b=ln_b,
        w_gate=w_gate, w_up=w_up, w_down=w_down,
        w_in_t=jnp.swapaxes(w_in, 1, 2),
        router=[_router_weights(w_router_group[l], b_router_group[l], w_router_expert[l], b_router_expert[l])
                for l in range(depth)],
    )

    attn_t = _tile(sp, 512)
    kk = jnp.arange(attn_t, dtype=jnp.int32)[:, None]
    qq = jnp.arange(attn_t, dtype=jnp.int32)[None, :]
    assert attn_t >= MAX_DISTANCE
    sub = _t5_bucket(qq + attn_t - kk)
    diag = jnp.where(qq >= kk, _t5_bucket(qq - kk), -1)
    bias_t = _bias_tiles(rel_bias, jnp.stack([sub, diag]))

    dec_bias = _decode_bias(rel_bias, psz, past_len)

    xp = x_prompt.reshape(bp * sp, d)
    xp_bf = xp.astype(BF16)
    xs = x_sample.reshape(bs, d)
    st_p, st_s = [], []
    for l in range(depth):
        xp, xp_bf, st = _prompt_layer(l, xp, xp_bf, bp, sp, bias_t, attn_t, 256, params)
        st_p.append(st)
        xs, st = _sample_layer(l, xs, state_conv[l], state_mlstm_c[l], state_mlstm_n[l], state_mlstm_m[l],
                               cache_k, cache_v, page_table, dec_bias, params)
        st_s.append(st)

    stack = lambda sts, i: jnp.stack([s[i] for s in sts])
    return (xp.reshape(bp, sp, d), xs.reshape(bs, ss, d),
            stack(st_p, 0), stack(st_p, 1), stack(st_s, 0), stack(st_s, 1),
            stack(st_p, 2), stack(st_s, 2),
            stack(st_p, 3), stack(st_s, 3),
            stack(st_p, 4), stack(st_s, 4),
            stack(st_p, 5), stack(st_s, 5))


def _decode_bias(rel_bias, psz, past_len):
    row = jnp.arange(psz * H_DA, dtype=jnp.int32)
    bk_last = _t5_bucket(past_len - (past_len - psz + row // H_DA))
    bk = jnp.stack([jnp.full_like(row, N_BUCKETS - 1), bk_last, jnp.zeros_like(row)])
    dec = _bias_tiles(rel_bias, bk.reshape(3, 1, psz * H_DA))[:, :, 0, :]
    own_head = (row % H_DA)[None, :] == jnp.arange(H_DA, dtype=jnp.int32)[:, None]
    dec_rows = jnp.sum(jnp.where(own_head[:, None, :], dec, 0.0), axis=0)
    return dec_rows[0:2], dec[:, 2, 0:1]
```
